```python
import math
import jax, jax.numpy as jnp
from jax import lax
import numpy as np

D_MODEL = 2048
BATCH = 1
SEQ = 8192
DEPTH = 2

HEAD_DIM = 128
D_MIX = D_MODEL
N_DIFF_HEADS = 8
DIFF_QK_DIM = HEAD_DIM // 2
N_SB_HEADS = 8
D_DIFF = N_DIFF_HEADS * HEAD_DIM
D_SB = N_SB_HEADS * HEAD_DIM
D_IN = 3 * D_DIFF + 3 * D_SB
N_EXPERTS = 32
TOP_K = 4
D_FF = D_MODEL
SWIGLU_LIMIT = 7.0
SWIGLU_ALPHA = 1.702
ROPE_THETA = 10000.0
Q_BLOCK = 128
TOKEN_BLOCK = 128
RMS_EPS = 1e-5
N_MOD = 6

kernel_name = "hybrid_diffattn_stickbreak_moe_adaln"


def rmsnorm(x, w, eps=RMS_EPS):
    xf = x.astype(jnp.float32)
    y = xf * lax.rsqrt(jnp.mean(xf * xf, axis=-1, keepdims=True) + eps)
    return (y * w.astype(jnp.float32)).astype(x.dtype)


def lambda_init_fn(layer_idx):
    return 0.8 - 0.6 * math.exp(-0.3 * layer_idx)


def rope(x, positions):
    d = x.shape[-1]
    half = d // 2
    inv_freq = ROPE_THETA ** (-jnp.arange(half, dtype=jnp.float32) / half)
    ang = positions.astype(jnp.float32)[..., None] * inv_freq
    ang = ang.reshape(ang.shape[:2] + (1,) * (x.ndim - 3) + (half,))
    cos, sin = jnp.cos(ang), jnp.sin(ang)
    xf = x.astype(jnp.float32)
    x1, x2 = xf[..., :half], xf[..., half:]
    return jnp.concatenate([x1 * cos - x2 * sin, x2 * cos + x1 * sin], axis=-1)


def diff_attention(q, k, v, lam):
    B, S, H = v.shape[:3]
    nb = S // Q_BLOCK
    scale = DIFF_QK_DIM ** -0.5
    qh = q.transpose(0, 2, 3, 1, 4).astype(jnp.float32) * scale
    kh = k.transpose(0, 2, 3, 1, 4).astype(jnp.float32)
    vh = v.transpose(0, 2, 1, 3).astype(jnp.float32)
    kidx = jnp.arange(S)

    def block(i):
        qb = lax.dynamic_slice_in_dim(qh, i * Q_BLOCK, Q_BLOCK, axis=3)
        s = jnp.einsum('bhpqd,bhpkd->bhpqk', qb, kh)
        qidx = i * Q_BLOCK + jnp.arange(Q_BLOCK)
        causal = kidx[None, :] <= qidx[:, None]
        p = jax.nn.softmax(jnp.where(causal, s, -jnp.inf), axis=-1)
        a = p[:, :, 0] - lam * p[:, :, 1]
        return jnp.einsum('bhqk,bhkd->bhqd', a, vh)

    out = lax.map(block, jnp.arange(nb))
    return out.transpose(1, 2, 0, 3, 4).reshape(B, H, S, HEAD_DIM)


def stick_breaking_attention(q, k, v):
    B, S, H, d = q.shape
    nb = S // Q_BLOCK
    scale = d ** -0.5
    qh = q.transpose(0, 2, 1, 3).astype(jnp.float32) * scale
    kh = k.transpose(0, 2, 1, 3).astype(jnp.float32)
    vh = v.transpose(0, 2, 1, 3).astype(jnp.float32)
    kidx = jnp.arange(S)

    def block(i):
        qb = lax.dynamic_slice_in_dim(qh, i * Q_BLOCK, Q_BLOCK, axis=2)
        z = jnp.einsum('bhqd,bhkd->bhqk', qb, kh)
        qidx = i * Q_BLOCK + jnp.arange(Q_BLOCK)
        strict = kidx[None, :] < qidx[:, None]
        log_1m = jnp.where(strict, jax.nn.log_sigmoid(-z), 0.0)
        rem = lax.cumsum(log_1m, axis=3, reverse=True) - log_1m
        a = jnp.where(strict, jnp.exp(jax.nn.log_sigmoid(z) + rem), 0.0)
        return jnp.einsum('bhqk,bhkd->bhqd', a, vh)

    out = lax.map(block, jnp.arange(nb))
    return out.transpose(1, 2, 0, 3, 4).reshape(B, H, S, d)


def moe_ffn(h, router_w, router_b, w_gate_up, b_gate_up, w_down, b_down):
    B, S, D = h.shape
    n = B * S
    ht = h.reshape(n, D)
    logits = jnp.matmul(ht, router_w).astype(jnp.float32) + router_b.astype(jnp.float32)
    top_v, top_i = lax.top_k(logits, TOP_K)
    gates = jax.nn.softmax(top_v, axis=-1)
    combine = jnp.sum(jax.nn.one_hot(top_i, N_EXPERTS, dtype=jnp.float32) * gates[..., None], axis=1)

    def token_block(args):
        xc, cc = args
        gu = jnp.einsum('nd,edf->nef', xc, w_gate_up) + b_gate_up
        gate = jnp.minimum(gu[..., 0::2], SWIGLU_LIMIT)
        up = jnp.clip(gu[..., 1::2], -SWIGLU_LIMIT, SWIGLU_LIMIT)
        act = gate * jax.nn.sigmoid(SWIGLU_ALPHA * gate) * (up + 1.0)
        y = jnp.einsum('nef,efd->ned', act, w_down) + b_down
        return jnp.einsum('ne,ned->nd', cc.astype(y.dtype), y)

    out = lax.map(token_block, (ht.reshape(-1, TOKEN_BLOCK, D), combine.reshape(-1, TOKEN_BLOCK, N_EXPERTS)))
    return out.reshape(B, S, D).astype(h.dtype)


def setup_inputs(seed: int = 0) -> dict:
    key = jax.random.key(seed)
    ks = jax.random.split(key, 24)
    f32 = jnp.float32
    nrm = lambda k, shape, s: jax.random.normal(k, shape, f32) * s
    x = jax.random.normal(ks[0], (BATCH, SEQ, D_MODEL), f32)
    c = jax.random.normal(ks[1], (BATCH, D_MODEL), f32)
    steps = jax.random.randint(ks[2], (BATCH, SEQ), 1, 3, dtype=jnp.int32)
    positions = (jnp.cumsum(steps, axis=1) - 1).astype(jnp.int32)
    return {
        "x": x,
        "c": c,
        "positions": positions,
        "ada_w": nrm(ks[3], (DEPTH, D_MODEL, N_MOD * D_MODEL), 0.5 * D_MODEL ** -0.5),
        "ada_b": nrm(ks[4], (DEPTH, N_MOD * D_MODEL), 0.02),
        "norm_mix": 1.0 + nrm(ks[5], (DEPTH, D_MODEL), 0.02),
        "w_in": nrm(ks[6], (DEPTH, D_MODEL, D_IN), D_MODEL ** -0.5),
        "w_out": nrm(ks[7], (DEPTH, D_MIX, D_MODEL), D_MIX ** -0.5),
        "lambda_q1": nrm(ks[8], (DEPTH, DIFF_QK_DIM), 0.1),
        "lambda_k1": nrm(ks[9], (DEPTH, DIFF_QK_DIM), 0.1),
        "lambda_q2": nrm(ks[10], (DEPTH, DIFF_QK_DIM), 0.1),
        "lambda_k2": nrm(ks[11], (DEPTH, DIFF_QK_DIM), 0.1),
        "subln_w": 1.0 + nrm(ks[12], (DEPTH, HEAD_DIM), 0.02),
        "sb_norm_w": 1.0 + nrm(ks[13], (DEPTH, HEAD_DIM), 0.02),
        "norm_ffn": 1.0 + nrm(ks[14], (DEPTH, D_MODEL), 0.02),
        "router_w": nrm(ks[15], (DEPTH, D_MODEL, N_EXPERTS), D_MODEL ** -0.5),
        "router_b": nrm(ks[16], (DEPTH, N_EXPERTS), 0.01),
        "w_gate_up": nrm(ks[17], (DEPTH, N_EXPERTS, D_MODEL, 2 * D_FF), D_MODEL ** -0.5),
        "b_gate_up": nrm(ks[18], (DEPTH, N_EXPERTS, 2 * D_FF), 0.02),
        "w_down": nrm(ks[19], (DEPTH, N_EXPERTS, D_FF, D_MODEL), D_FF ** -0.5),
        "b_down": nrm(ks[20], (DEPTH, N_EXPERTS, D_MODEL), 0.02),
        "norm_final": 1.0 + nrm(ks[21], (D_MODEL,), 0.02),
    }


def reference(x, c, positions, ada_w, ada_b, norm_mix, w_in, w_out, lambda_q1, lambda_k1, lambda_q2, lambda_k2,
              subln_w, sb_norm_w, norm_ffn, router_w, router_b, w_gate_up, b_gate_up, w_down, b_down, norm_final):
    B, S, D = x.shape
    c_act = jax.nn.silu(c)
    for l in range(DEPTH):
        lam_init = lambda_init_fn(l)
        mod = jnp.matmul(c_act, ada_w[l]) + ada_b[l]
        shift_a, scale_a, gate_a, shift_f, scale_f, gate_f = [m[:, None, :] for m in jnp.split(mod, N_MOD, axis=-1)]

        h = rmsnorm(x, norm_mix[l]) * (1.0 + scale_a) + shift_a
        proj = jnp.matmul(h, w_in[l])
        dq, dk, dv, sq, sk, sv = jnp.split(proj, [D_DIFF, 2 * D_DIFF, 3 * D_DIFF, 3 * D_DIFF + D_SB, 3 * D_DIFF + 2 * D_SB], axis=-1)

        dq = rope(dq.reshape(B, S, N_DIFF_HEADS, 2, DIFF_QK_DIM), positions)
        dk = rope(dk.reshape(B, S, N_DIFF_HEADS, 2, DIFF_QK_DIM), positions)
        dv = dv.reshape(B, S, N_DIFF_HEADS, HEAD_DIM)
        lam = (jnp.exp(jnp.sum(lambda_q1[l].astype(jnp.float32) * lambda_k1[l].astype(jnp.float32)))
               - jnp.exp(jnp.sum(lambda_q2[l].astype(jnp.float32) * lambda_k2[l].astype(jnp.float32)))
               + lam_init)
        a_out = diff_attention(dq, dk, dv, lam)
        a_out = rmsnorm(a_out, subln_w[l]) * (1.0 - lam_init)
        a_out = a_out.transpose(0, 2, 1, 3).reshape(B, S, D_DIFF)

        b_out = stick_breaking_attention(sq.reshape(B, S, N_SB_HEADS, HEAD_DIM),
                                         sk.reshape(B, S, N_SB_HEADS, HEAD_DIM),
                                         sv.reshape(B, S, N_SB_HEADS, HEAD_DIM))
        b_out = rmsnorm(b_out, sb_norm_w[l]).transpose(0, 2, 1, 3).reshape(B, S, D_SB)

        mix = jnp.concatenate([a_out.astype(x.dtype), b_out.astype(x.dtype)], axis=-1)
        x = x + gate_a * jnp.matmul(mix, w_out[l])

        h = rmsnorm(x, norm_ffn[l]) * (1.0 + scale_f) + shift_f
        x = x + gate_f * moe_ffn(h, router_w[l], router_b[l], w_gate_up[l], b_gate_up[l], w_down[l], b_down[l])
    return rmsnorm(x, norm_final)
```

```python
import functools
import math

import jax
import jax.numpy as jnp
from jax import lax
from jax.experimental import pallas as pl
from jax.experimental.pallas import tpu as pltpu

F32 = jnp.float32
BF16 = jnp.bfloat16

HEAD_DIM = 128
DIFF_QK_DIM = HEAD_DIM // 2
N_MOD = 6
TOP_K = 4
SWIGLU_LIMIT = 7.0
SWIGLU_ALPHA = 1.702
ROPE_THETA = 10000.0
RMS_EPS = 1e-5

V7X_LANES = 128
V7X_VMEM_BYTES = 64 * 1024 * 1024

TM_PROJ = 512
TQ_ATTN = 256
TM_OUT = 256
TM_RANK = 256
TM_MOE = 256
TM_COMB = 256
MOE_CHUNK = 1024
MOD_CHUNK = 1024
NEG_BIG = -1e30
SB_EXIT_LOG = -100.0

_NT_DIMS = (((1,), (1,)), ((), ()))


def _params(n_axes, vmem_bytes):
    return pltpu.CompilerParams(
        dimension_semantics=("arbitrary",) * n_axes,
        vmem_limit_bytes=min(int(vmem_bytes), V7X_VMEM_BYTES - 8 * 1024 * 1024),
    )


def _rms(x):
    return x * lax.rsqrt(jnp.mean(x * x, axis=-1, keepdims=True) + RMS_EPS)


def _mod_kernel(c_ref, w_ref, b_ref, o_ref):
    c = c_ref[...]
    ca = c * (1.0 / (1.0 + jnp.exp(-c)))
    o_ref[...] = jnp.sum(ca * w_ref[...], axis=0, keepdims=True) + b_ref[...]


def _ada_mod(c_col, ada_w, ada_b):
    depth, d, n = ada_w.shape
    nc = min(MOD_CHUNK, n)
    return pl.pallas_call(
        _mod_kernel,
        grid=(depth, n // nc),
        in_specs=[
            pl.BlockSpec((d, 1), lambda l, j: (0, 0)),
            pl.BlockSpec((None, d, nc), lambda l, j: (l, 0, j)),
            pl.BlockSpec((None, 1, nc), lambda l, j: (l, 0, j)),
        ],
        out_specs=pl.BlockSpec((None, 1, nc), lambda l, j: (l, 0, j)),
        out_shape=jax.ShapeDtypeStruct((depth, 1, n), F32),
        compiler_params=_params(2, 3 * d * nc * 4 + 4 * d * V7X_LANES * 4),
        name="ada_mod",
    )(c_col, ada_w, ada_b.reshape(depth, 1, n))


def _rope_kernel(pos_ref, invf_ref, sign_ref, cos_ref, sin_ref):
    ang = pos_ref[...] * invf_ref[...]
    cos_ref[...] = jnp.cos(ang)
    sin_ref[...] = jnp.sin(ang) * sign_ref[...]


def _rope_tables(pos_col):
    s = pos_col.shape[0]
    half = DIFF_QK_DIM // 2
    inv_freq = ROPE_THETA ** (-jnp.arange(half, dtype=F32) / half)
    reps = V7X_LANES // half
    invf = jnp.tile(inv_freq, reps).reshape(1, V7X_LANES)
    sign = jnp.tile(jnp.concatenate([-jnp.ones((half,), F32), jnp.ones((half,), F32)]), reps // 2)
    ts = min(1024, s)
    tab = jax.ShapeDtypeStruct((s, V7X_LANES), F32)
    return pl.pallas_call(
        _rope_kernel,
        grid=(s // ts,),
        in_specs=[
            pl.BlockSpec((ts, 1), lambda i: (i, 0)),
            pl.BlockSpec((1, V7X_LANES), lambda i: (0, 0)),
            pl.BlockSpec((1, V7X_LANES), lambda i: (0, 0)),
        ],
        out_specs=[pl.BlockSpec((ts, V7X_LANES), lambda i: (i, 0))] * 2,
        out_shape=[tab, tab],
        compiler_params=_params(1, 32 * 1024 * 1024),
        name="rope_tables",
    )(pos_col, invf, sign.reshape(1, V7X_LANES))


def _in_proj_kernel(x_ref, nw_ref, sc_ref, sh_ref, w_ref, cos_ref, sin_ref, o_ref, h_scr, *, n_heads):
    j = pl.program_id(1)

    @pl.when(j == 0)
    def _():
        h = _rms(x_ref[...]) * nw_ref[...]
        h_scr[...] = (h * (1.0 + sc_ref[...]) + sh_ref[...]).astype(BF16)

    res = jnp.dot(h_scr[...], w_ref[...], preferred_element_type=F32)

    def store(scale, rotary):
        if rotary:
            cos = cos_ref[...]
            sin = sin_ref[...]
            lane = lax.broadcasted_iota(jnp.int32, cos.shape, 1)
            first_half = (lane % DIFF_QK_DIM) < (DIFF_QK_DIM // 2)
        for g in range(n_heads):
            blk = res[:, g * HEAD_DIM:(g + 1) * HEAD_DIM]
            if rotary:
                fwd = pltpu.roll(blk, HEAD_DIM - DIFF_QK_DIM // 2, 1)
                bwd = pltpu.roll(blk, DIFF_QK_DIM // 2, 1)
                blk = blk * cos + jnp.where(first_half, fwd, bwd) * sin
            if scale != 1.0:
                blk = blk * scale
            o_ref[g] = blk.astype(BF16)

    pl.when(j == 0)(lambda: store(DIFF_QK_DIM ** -0.5, True))
    pl.when(j == 1)(lambda: store(1.0, True))
    pl.when(j == 3)(lambda: store(HEAD_DIM ** -0.5, False))
    pl.when((j == 2) | (j >= 4))(lambda: store(1.0, False))


def _in_proj(x, mod, layer, norm_w, w_in, cos, sin):
    s, d = x.shape
    gw = w_in.shape[-1] // 6
    n_heads = gw // HEAD_DIM
    tm = min(TM_PROJ, s)
    vmem = 2 * tm * d * 4 + tm * d * 2 + 2 * d * gw * 2 + 2 * tm * gw * 2 + 3 * tm * gw * 4 + (4 << 20)
    return pl.pallas_call(
        functools.partial(_in_proj_kernel, n_heads=n_heads),
        grid=(s // tm, 6),
        in_specs=[
            pl.BlockSpec((tm, d), lambda i, j: (i, 0)),
            pl.BlockSpec((1, d), lambda i, j: (0, 0)),
            pl.BlockSpec((None, 1, d), lambda i, j: (layer, 0, 1)),
            pl.BlockSpec((None, 1, d), lambda i, j: (layer, 0, 0)),
            pl.BlockSpec((None, d, gw), lambda i, j: (layer, 0, j)),
            pl.BlockSpec((tm, V7X_LANES), lambda i, j: (i, 0)),
            pl.BlockSpec((tm, V7X_LANES), lambda i, j: (i, 0)),
        ],
        out_specs=pl.BlockSpec((n_heads, tm, HEAD_DIM), lambda i, j: (j, i, 0)),
        out_shape=jax.ShapeDtypeStruct((6 * n_heads, s, HEAD_DIM), BF16),
        scratch_shapes=[pltpu.VMEM((tm, d), BF16)],
        compiler_params=_params(2, vmem),
        name="in_proj",
    )(x, norm_w, mod, mod, w_in, cos, sin)


def _diff_attn_kernel(lam_ref, q_ref, k_ref, v_ref, w_ref, o_ref, *, tq, out_scale):
    qi = pl.program_id(1)
    q = q_ref[...]
    lane = lax.broadcasted_iota(jnp.int32, q.shape, 1)
    zero = jnp.zeros_like(q)
    q_sub = (jnp.where(lane < DIFF_QK_DIM, q, zero), jnp.where(lane >= DIFF_QK_DIM, q, zero))

    def step(ki, carry, diag):
        start = pl.multiple_of(ki * tq, tq)
        k = k_ref[pl.ds(start, tq), :]
        v = v_ref[pl.ds(start, tq), :]
        out = []
        for p in range(2):
            m, l, acc = carry[p]
            s = lax.dot_general(q_sub[p], k, _NT_DIMS, preferred_element_type=F32)
            if diag:
                row = lax.broadcasted_iota(jnp.int32, s.shape, 0)
                col = lax.broadcasted_iota(jnp.int32, s.shape, 1)
                s = jnp.where(col <= row, s, NEG_BIG)
            m_new = jnp.maximum(m, jnp.max(s, axis=-1, keepdims=True))
            alpha = jnp.exp(m - m_new)
            pe = jnp.exp(s - m_new)
            l = alpha * l + jnp.sum(pe, axis=-1, keepdims=True)
            acc = alpha * acc + jnp.dot(pe.astype(BF16), v, preferred_element_type=F32)
            out.append((m_new, l, acc))
        return tuple(out)

    one = (jnp.full((tq, 1), NEG_BIG, F32), jnp.zeros((tq, 1), F32), jnp.zeros((tq, HEAD_DIM), F32))
    carry = lax.fori_loop(0, qi, lambda ki, c: step(ki, c, False), (one, one))
    (_, l0, a0), (_, l1, a1) = step(qi, carry, True)
    a = a0 / l0 - lam_ref[0] * (a1 / l1)
    o_ref[...] = (_rms(a) * (w_ref[...] * out_scale)).astype(BF16)


def _diff_attn(qkv, lam, subln_w, out_scale):
    n_heads = qkv.shape[0] // 6
    s = qkv.shape[1]
    tq = min(TQ_ATTN, s)
    vmem = 4 * s * HEAD_DIM * 2 + 16 * tq * tq * 4 + (8 << 20)
    return pl.pallas_call(
        functools.partial(_diff_attn_kernel, tq=tq, out_scale=out_scale),
        grid=(n_heads, s // tq),
        in_specs=[
            pl.BlockSpec(memory_space=pltpu.SMEM),
            pl.BlockSpec((None, tq, HEAD_DIM), lambda h, i: (h, i, 0)),
            pl.BlockSpec((None, s, HEAD_DIM), lambda h, i: (n_heads + h, 0, 0)),
            pl.BlockSpec((None, s, HEAD_DIM), lambda h, i: (2 * n_heads + h, 0, 0)),
            pl.BlockSpec((1, HEAD_DIM), lambda h, i: (0, 0)),
        ],
        out_specs=pl.BlockSpec((None, tq, HEAD_DIM), lambda h, i: (h, i, 0)),
        out_shape=jax.ShapeDtypeStruct((n_heads, s, HEAD_DIM), BF16),
        compiler_params=_params(2, vmem),
        name="diff_attn",
    )(lam, qkv, qkv, qkv, subln_w)


def _sb_attn_kernel(q_ref, k_ref, v_ref, w_ref, o_ref, *, tq):
    qi = pl.program_id(1)
    q = q_ref[...]
    row = lax.broadcasted_iota(jnp.int32, (tq, tq), 0)
    col = lax.broadcasted_iota(jnp.int32, (tq, tq), 1)
    strict = col < row
    after = jnp.where(row > col, 1.0, 0.0).astype(BF16)

    def block(ki, rem, acc, diag):
        start = pl.multiple_of(ki * tq, tq)
        k = k_ref[pl.ds(start, tq), :]
        v = v_ref[pl.ds(start, tq), :]
        z = lax.dot_general(q, k, _NT_DIMS, preferred_element_type=F32)
        t = jnp.log1p(jnp.exp(-jnp.abs(z)))
        log_beta = jnp.minimum(z, 0.0) - t
        log_1m = -jnp.maximum(z, 0.0) - t
        if diag:
            log_1m = jnp.where(strict, log_1m, 0.0)
        hi = log_1m.astype(BF16)
        lo = (log_1m - hi.astype(F32)).astype(BF16)
        later = (jnp.dot(hi, after, preferred_element_type=F32)
                 + jnp.dot(lo, after, preferred_element_type=F32))
        a = jnp.exp(log_beta + later + rem)
        if diag:
            a = jnp.where(strict, a, 0.0)
        acc = acc + jnp.dot(a.astype(BF16), v, preferred_element_type=F32)
        rem = rem + jnp.sum(log_1m, axis=-1, keepdims=True)
        return rem, acc

    rem, acc = block(qi, jnp.zeros((tq, 1), F32), jnp.zeros((tq, HEAD_DIM), F32), True)

    def cond(c):
        ki, rem, _ = c
        return jnp.logical_and(ki >= 0, jnp.max(rem) > SB_EXIT_LOG)

    def body(c):
        ki, rem, acc = c
        rem, acc = block(ki, rem, acc, False)
        return ki - 1, rem, acc

    _, _, acc = lax.while_loop(cond, body, (qi - 1, rem, acc))
    o_ref[...] = (_rms(acc) * w_ref[...]).astype(BF16)


def _sb_attn(qkv, norm_w):
    n_heads = qkv.shape[0] // 6
    s = qkv.shape[1]
    tq = min(TQ_ATTN, s)
    vmem = 4 * s * HEAD_DIM * 2 + 16 * tq * tq * 4 + (8 << 20)
    return pl.pallas_call(
        functools.partial(_sb_attn_kernel, tq=tq),
        grid=(n_heads, s // tq),
        in_specs=[
            pl.BlockSpec((None, tq, HEAD_DIM), lambda h, i: (3 * n_heads + h, i, 0)),
            pl.BlockSpec((None, s, HEAD_DIM), lambda h, i: (4 * n_heads + h, 0, 0)),
            pl.BlockSpec((None, s, HEAD_DIM), lambda h, i: (5 * n_heads + h, 0, 0)),
            pl.BlockSpec((1, HEAD_DIM), lambda h, i: (0, 0)),
        ],
        out_specs=pl.BlockSpec((None, tq, HEAD_DIM), lambda h, i: (h, i, 0)),
        out_shape=jax.ShapeDtypeStruct((n_heads, s, HEAD_DIM), BF16),
        compiler_params=_params(2, vmem),
        name="sb_attn",
    )(qkv, qkv, qkv, norm_w)


def _out_router_kernel(a_ref, b_ref, w_ref, x_ref, ga_ref, nw_ref, sc_ref, sh_ref, rwh_ref, rwl_ref, rb_ref,
                       x1_ref, hp_ref, eidx_ref, gate_ref, sel_ref, *, n_heads):
    mix = jnp.concatenate([a_ref[g] for g in range(n_heads)] + [b_ref[g] for g in range(n_heads)], axis=-1)
    y = jnp.dot(mix, w_ref[...], preferred_element_type=F32)
    x1 = x_ref[...] + ga_ref[...] * y
    x1_ref[...] = x1
    h = _rms(x1) * nw_ref[...]
    h = h * (1.0 + sc_ref[...]) + sh_ref[...]
    half = h.shape[-1] // 2
    hp_ref[...] = pltpu.pack_elementwise([h[:, :half], h[:, half:]], packed_dtype=BF16)

    h_hi = h.astype(BF16)
    h_lo = (h - h_hi.astype(F32)).astype(BF16)
    logits = (jnp.dot(h_hi, rwh_ref[...], preferred_element_type=F32)
              + jnp.dot(h_lo, rwh_ref[...], preferred_element_type=F32)
              + jnp.dot(h_hi, rwl_ref[...], preferred_element_type=F32)) + rb_ref[...]

    lane = lax.broadcasted_iota(jnp.int32, logits.shape, 1).astype(F32)
    vals, idxs = [], []
    sel = jnp.zeros(logits.shape, F32)
    for _ in range(TOP_K):
        m = jnp.max(logits, axis=-1, keepdims=True)
        idx = jnp.min(jnp.where(logits == m, lane, float(V7X_LANES)), axis=-1, keepdims=True)
        hit = lane == idx
        vals.append(m)
        idxs.append(idx)
        sel = jnp.where(hit, 1.0, sel)
        logits = jnp.where(hit, -3e38, logits)
    exps = [jnp.exp(v - vals[0]) for v in vals]
    denom = exps[0] + exps[1] + exps[2] + exps[3]
    eidx = jnp.zeros(logits.shape, F32)
    gate = jnp.zeros(logits.shape, F32)
    for kk in range(TOP_K):
        eidx = jnp.where(lane == float(kk), idxs[kk], eidx)
        gate = jnp.where(lane == float(kk), exps[kk] / denom, gate)
    eidx_ref[...] = eidx.astype(jnp.int32)
    gate_ref[...] = gate
    sel_ref[...] = sel.astype(BF16)


def _out_router(a_mix, b_mix, w_out, x, mod, layer, norm_w, rw_hi, rw_lo, rb):
    n_heads, s, _ = a_mix.shape
    d = x.shape[1]
    tm = min(TM_OUT, s)
    vmem = 2 * d * d * 2 + 10 * tm * d * 4 + (8 << 20)
    row = lambda i: (i, 0)
    fixed = lambda i: (0, 0)
    lanes = jax.ShapeDtypeStruct((s, V7X_LANES), F32)
    return pl.pallas_call(
        functools.partial(_out_router_kernel, n_heads=n_heads),
        grid=(s // tm,),
        in_specs=[
            pl.BlockSpec((n_heads, tm, HEAD_DIM), lambda i: (0, i, 0)),
            pl.BlockSpec((n_heads, tm, HEAD_DIM), lambda i: (0, i, 0)),
            pl.BlockSpec((None, d, d), lambda i: (layer, 0, 0)),
            pl.BlockSpec((tm, d), row),
            pl.BlockSpec((None, 1, d), lambda i: (layer, 0, 2)),
            pl.BlockSpec((1, d), fixed),
            pl.BlockSpec((None, 1, d), lambda i: (layer, 0, 4)),
            pl.BlockSpec((None, 1, d), lambda i: (layer, 0, 3)),
            pl.BlockSpec((d, V7X_LANES), fixed),
            pl.BlockSpec((d, V7X_LANES), fixed),
            pl.BlockSpec((1, V7X_LANES), fixed),
        ],
        out_specs=[
            pl.BlockSpec((tm, d), row),
            pl.BlockSpec((tm, d // 2), row),
            pl.BlockSpec((tm, V7X_LANES), row),
            pl.BlockSpec((tm, V7X_LANES), row),
            pl.BlockSpec((tm, V7X_LANES), row),
        ],
        out_shape=[
            jax.ShapeDtypeStruct((s, d), F32),
            jax.ShapeDtypeStruct((s, d // 2), jnp.uint32),
            jax.ShapeDtypeStruct((s, V7X_LANES), jnp.int32),
            lanes,
            jax.ShapeDtypeStruct((s, V7X_LANES), BF16),
        ],
        compiler_params=_params(1, vmem),
        name="out_router",
    )(a_mix, b_mix, w_out, x, mod, norm_w, mod, mod, rw_hi, rw_lo, rb)


def _rank_kernel(sel_ref, rank_ref, count_ref, run_scr):
    i = pl.program_id(0)

    @pl.when(i == 0)
    def _():
        run_scr[...] = jnp.zeros_like(run_scr)

    sel = sel_ref[...]
    tm = sel.shape[0]
    row = lax.broadcasted_iota(jnp.int32, (tm, tm), 0)
    col = lax.broadcasted_iota(jnp.int32, (tm, tm), 1)
    before = jnp.where(col < row, 1.0, 0.0).astype(BF16)
    run = run_scr[...]
    rank_ref[...] = jnp.dot(before, sel, preferred_element_type=F32) + run
    run = run + jnp.sum(sel.astype(F32), axis=0, keepdims=True)
    run_scr[...] = run
    count_ref[...] = run


def _rank(sel):
    s = sel.shape[0]
    tm = min(TM_RANK, s)
    return pl.pallas_call(
        _rank_kernel,
        grid=(s // tm,),
        in_specs=[pl.BlockSpec((tm, V7X_LANES), lambda i: (i, 0))],
        out_specs=[pl.BlockSpec((tm, V7X_LANES), lambda i: (i, 0)),
                   pl.BlockSpec((1, V7X_LANES), lambda i: (0, 0))],
        out_shape=[jax.ShapeDtypeStruct((s, V7X_LANES), F32), jax.ShapeDtypeStruct((1, V7X_LANES), F32)],
        scratch_shapes=[pltpu.VMEM((1, V7X_LANES), F32)],
        compiler_params=_params(1, 16 * 1024 * 1024),
        name="route_rank",
    )(sel)


def _scatter_kernel(pos_ref, h_ref, xs_in_ref, xs_ref, sem, *, tm):
    del xs_in_ref

    def issue(r, _):
        for kk in range(TOP_K):
            dst = pos_ref[r * TOP_K + kk]
            pltpu.make_async_copy(h_ref.at[pl.ds(r, 1)], xs_ref.at[pl.ds(dst, 1)], sem).start()
        return ()

    lax.fori_loop(0, tm, issue, ())
    for _ in range(TOP_K):
        pltpu.make_async_copy(h_ref, xs_ref.at[pl.ds(0, tm)], sem).wait()


def _scatter_rows(pos_flat, hp, xs_zero):
    s, w = hp.shape
    tm = min(TM_COMB, s)
    return pl.pallas_call(
        functools.partial(_scatter_kernel, tm=tm),
        grid=(s // tm,),
        in_specs=[
            pl.BlockSpec((tm * TOP_K,), lambda i: (i,), memory_space=pltpu.SMEM),
            pl.BlockSpec((tm, w), lambda i: (i, 0)),
            pl.BlockSpec(memory_space=pl.ANY),
        ],
        out_specs=pl.BlockSpec(memory_space=pl.ANY),
        out_shape=jax.ShapeDtypeStruct(xs_zero.shape, xs_zero.dtype),
        scratch_shapes=[pltpu.SemaphoreType.DMA(())],
        input_output_aliases={2: 0},
        compiler_params=_params(1, 16 * 1024 * 1024),
        name="moe_scatter",
    )(pos_flat, hp, xs_zero)


def _moe_up_kernel(tile_ref, oc_ref, wc_ref, exp_ref, first_ref, valid_ref, x_ref, w_ref, b_ref, o_ref, wb_scr):
    del tile_ref, oc_ref, wc_ref, exp_ref
    s = pl.program_id(0)

    @pl.when(first_ref[s] == 1)
    def _():
        wb_scr[...] = w_ref[...].astype(BF16)

    @pl.when(valid_ref[s] == 1)
    def _():
        xp = x_ref[...]
        half = xp.shape[-1]
        x_lo = pltpu.unpack_elementwise(xp, index=0, packed_dtype=BF16, unpacked_dtype=F32).astype(BF16)
        x_hi = pltpu.unpack_elementwise(xp, index=1, packed_dtype=BF16, unpacked_dtype=F32).astype(BF16)
        gu = (jnp.dot(x_lo, wb_scr[0:half, :], preferred_element_type=F32)
              + jnp.dot(x_hi, wb_scr[half:2 * half, :], preferred_element_type=F32)) + b_ref[...]
        gate = jnp.minimum(gu, SWIGLU_LIMIT)
        glu = gate * (1.0 / (1.0 + jnp.exp(-SWIGLU_ALPHA * gate)))
        up1 = jnp.clip(gu, -SWIGLU_LIMIT, SWIGLU_LIMIT) + 1.0
        pair = 2 * V7X_LANES
        r = lax.broadcasted_iota(jnp.int32, (pair, V7X_LANES), 0)
        c = lax.broadcasted_iota(jnp.int32, (pair, V7X_LANES), 1)
        even = jnp.where(r == 2 * c, 1.0, 0.0).astype(BF16)
        outs = []
        for q in range(gu.shape[-1] // pair):
            sl = slice(q * pair, (q + 1) * pair)
            prod = glu[:, sl] * pltpu.roll(up1[:, sl], pair - 1, 1)
            outs.append(jnp.dot(prod.astype(BF16), even, preferred_element_type=F32))
        o_ref[...] = jnp.concatenate(outs, axis=-1).astype(BF16)

    @pl.when(valid_ref[s] == 0)
    def _():
        o_ref[...] = jnp.zeros_like(o_ref)


def _moe_up(sched, xs, w_gate_up, b_gate_up, layer):
    p, half = xs.shape
    n_exp, d, f2 = w_gate_up.shape[1:]
    nc = min(MOE_CHUNK, f2)
    tm = TM_MOE
    n_steps = sched[0].shape[0]
    vmem = 2 * d * nc * 4 + d * nc * 2 + 2 * tm * half * 4 + 8 * tm * nc * 4 + (6 << 20)
    grid_spec = pltpu.PrefetchScalarGridSpec(
        num_scalar_prefetch=6,
        grid=(n_steps,),
        in_specs=[
            pl.BlockSpec((tm, half), lambda s, t, oc, wc, e, f, v: (t[s], 0)),
            pl.BlockSpec((None, None, d, nc), lambda s, t, oc, wc, e, f, v: (layer, e[s], 0, wc[s])),
            pl.BlockSpec((None, None, 1, nc), lambda s, t, oc, wc, e, f, v: (layer, e[s], 0, wc[s])),
        ],
        out_specs=pl.BlockSpec((tm, nc // 2), lambda s, t, oc, wc, e, f, v: (t[s], oc[s])),
        scratch_shapes=[pltpu.VMEM((d, nc), BF16)],
    )
    return pl.pallas_call(
        _moe_up_kernel,
        grid_spec=grid_spec,
        out_shape=jax.ShapeDtypeStruct((p, f2 // 2), BF16),
        compiler_params=_params(1, vmem),
        name="moe_up",
    )(*sched, xs, w_gate_up, b_gate_up.reshape(b_gate_up.shape[0], n_exp, 1, f2))


def _moe_down_kernel(tile_ref, oc_ref, wc_ref, exp_ref, first_ref, valid_ref, a_ref, w_ref, b_ref, o_ref, wb_scr):
    del tile_ref, oc_ref, wc_ref, exp_ref
    s = pl.program_id(0)

    @pl.when(first_ref[s] == 1)
    def _():
        wb_scr[...] = w_ref[...].astype(BF16)

    @pl.when(valid_ref[s] == 1)
    def _():
        o_ref[...] = jnp.dot(a_ref[...], wb_scr[...], preferred_element_type=F32) + b_ref[...]

    @pl.when(valid_ref[s] == 0)
    def _():
        o_ref[...] = jnp.zeros_like(o_ref)


def _moe_down(sched, act, w_down, b_down, layer):
    p, f = act.shape
    n_exp, _, d = w_down.shape[1:]
    nc = min(MOE_CHUNK, d)
    tm = TM_MOE
    n_steps = sched[0].shape[0]
    vmem = 2 * f * nc * 4 + f * nc * 2 + 2 * tm * f * 2 + 4 * tm * nc * 4 + (6 << 20)
    grid_spec = pltpu.PrefetchScalarGridSpec(
        num_scalar_prefetch=6,
        grid=(n_steps,),
        in_specs=[
            pl.BlockSpec((tm, f), lambda s, t, oc, wc, e, fi, v: (t[s], 0)),
            pl.BlockSpec((None, None, f, nc), lambda s, t, oc, wc, e, fi, v: (layer, e[s], 0, wc[s])),
            pl.BlockSpec((None, None, 1, nc), lambda s, t, oc, wc, e, fi, v: (layer, e[s], 0, wc[s])),
        ],
        out_specs=pl.BlockSpec((tm, nc), lambda s, t, oc, wc, e, fi, v: (t[s], oc[s])),
        scratch_shapes=[pltpu.VMEM((f, nc), BF16)],
    )
    return pl.pallas_call(
        _moe_down_kernel,
        grid_spec=grid_spec,
        out_shape=jax.ShapeDtypeStruct((p, d), F32),
        compiler_params=_params(1, vmem),
        name="moe_down",
    )(*sched, act, w_down, b_down.reshape(b_down.shape[0], n_exp, 1, d))


def _moe_schedule(counts, n_tiles, n_chunks):
    tm = TM_MOE
    n_exp = counts.shape[0]
    tiles_per = (counts + tm - 1) // tm
    tile_end = jnp.cumsum(tiles_per)
    tile_start = tile_end - tiles_per
    total = tile_end[-1]
    step = jnp.arange(n_tiles * n_chunks, dtype=jnp.int32)
    lin = step // n_chunks
    valid = lin < total
    expert = jnp.minimum(jnp.searchsorted(tile_end, lin, side="right"), n_exp - 1).astype(jnp.int32)
    rel = step - tile_start[expert] * n_chunks
    per = jnp.maximum(tiles_per[expert], 1)
    chunk = rel // per
    tile = tile_start[expert] + rel % per
    first = (rel % per) == 0
    last_expert = jnp.minimum(jnp.searchsorted(tile_end, total - 1, side="right"), n_exp - 1).astype(jnp.int32)
    tile = jnp.where(valid, tile, lin)
    out_chunk = jnp.where(valid, chunk, step % n_chunks)
    w_chunk = jnp.where(valid, chunk, n_chunks - 1)
    expert = jnp.where(valid, expert, last_expert)
    first = jnp.where(valid, first, False)
    i32 = lambda a: a.astype(jnp.int32)
    return (i32(tile), i32(out_chunk), i32(w_chunk), i32(expert), i32(first), i32(valid)), tile_start * tm


def _combine_kernel(pos_ref, ys_ref, x_ref, gate_ref, gf_ref, nw_ref, o_ref, buf, sem, *, tm, final):
    def issue(r, _):
        for kk in range(TOP_K):
            src = pos_ref[r * TOP_K + kk]
            pltpu.make_async_copy(ys_ref.at[pl.ds(src, 1)], buf.at[kk, pl.ds(r, 1)], sem).start()
        return ()

    lax.fori_loop(0, tm, issue, ())
    for kk in range(TOP_K):
        pltpu.make_async_copy(ys_ref.at[pl.ds(0, tm)], buf.at[kk], sem).wait()

    gates = gate_ref[...]
    y = gates[:, 0:1] * buf[0]
    for kk in range(1, TOP_K):
        y = y + gates[:, kk:kk + 1] * buf[kk]
    out = x_ref[...] + gf_ref[...] * y
    if final:
        out = _rms(out) * nw_ref[...]
    o_ref[...] = out


def _combine(pos_flat, ys, x1, gates, mod, layer, norm_final, final):
    s, d = x1.shape
    tm = min(TM_COMB, s)
    vmem = TOP_K * tm * d * 4 + 6 * tm * d * 4 + (6 << 20)
    return pl.pallas_call(
        functools.partial(_combine_kernel, tm=tm, final=final),
        grid=(s // tm,),
        in_specs=[
            pl.BlockSpec((tm * TOP_K,), lambda i: (i,), memory_space=pltpu.SMEM),
            pl.BlockSpec(memory_space=pl.ANY),
            pl.BlockSpec((tm, d), lambda i: (i, 0)),
            pl.BlockSpec((tm, V7X_LANES), lambda i: (i, 0)),
            pl.BlockSpec((None, 1, d), lambda i: (layer, 0, 5)),
            pl.BlockSpec((1, d), lambda i: (0, 0)),
        ],
        out_specs=pl.BlockSpec((tm, d), lambda i: (i, 0)),
        out_shape=jax.ShapeDtypeStruct((s, d), F32),
        scratch_shapes=[pltpu.VMEM((TOP_K, tm, d), F32), pltpu.SemaphoreType.DMA(())],
        compiler_params=_params(1, vmem),
        name="moe_combine",
    )(pos_flat, ys, x1, gates, mod, norm_final)


def kernel(x, c, positions, ada_w, ada_b, norm_mix, w_in, w_out, lambda_q1, lambda_k1, lambda_q2, lambda_k2,
           subln_w, sb_norm_w, norm_ffn, router_w, router_b, w_gate_up, b_gate_up, w_down, b_down, norm_final):
    batch, s, d = x.shape
    assert batch == 1, "the kernels are written for one sequence"
    depth = ada_w.shape[0]
    n_exp = router_w.shape[-1]
    assert n_exp <= V7X_LANES

    xc = x.reshape(s, d)
    mod = _ada_mod(c.reshape(d, 1), ada_w, ada_b)
    cos, sin = _rope_tables(positions.reshape(s, 1).astype(F32))
    w_in_b = w_in.astype(BF16)
    w_out_b = w_out.astype(BF16)

    n_tiles = (s * TOP_K) // TM_MOE + n_exp
    n_rows = n_tiles * TM_MOE
    pad = V7X_LANES - n_exp
    row = lambda a: a.reshape(1, -1)

    for l in range(depth):
        lam_init = 0.8 - 0.6 * math.exp(-0.3 * l)
        lam = (jnp.exp(jnp.sum(lambda_q1[l] * lambda_k1[l])) - jnp.exp(jnp.sum(lambda_q2[l] * lambda_k2[l]))
               + lam_init).reshape(1).astype(F32)

        qkv = _in_proj(xc, mod, l, row(norm_mix[l]), w_in_b, cos, sin)
        a_mix = _diff_attn(qkv, lam, row(subln_w[l]), 1.0 - lam_init)
        b_mix = _sb_attn(qkv, row(sb_norm_w[l]))

        rw = jnp.pad(router_w[l], ((0, 0), (0, pad)))
        rw_hi = rw.astype(BF16)
        rw_lo = (rw - rw_hi.astype(F32)).astype(BF16)
        rb = jnp.pad(router_b[l], (0, pad), constant_values=NEG_BIG).reshape(1, V7X_LANES)
        x1, hp, eidx, gates, sel = _out_router(a_mix, b_mix, w_out_b, xc, mod, l, row(norm_ffn[l]),
                                               rw_hi, rw_lo, rb)

        rank, counts = _rank(sel)
        counts = counts[0, :n_exp].astype(jnp.int32)
        n_chunks_up = w_gate_up.shape[-1] // min(MOE_CHUNK, w_gate_up.shape[-1])
        n_chunks_dn = w_down.shape[-1] // min(MOE_CHUNK, w_down.shape[-1])
        sched_up, row_start = _moe_schedule(counts, n_tiles, n_chunks_up)
        sched_dn, _ = _moe_schedule(counts, n_tiles, n_chunks_dn)
        slot = rank[:, :n_exp] + row_start.astype(F32)[None, :]
        pos = jnp.take_along_axis(slot, eidx[:, :TOP_K], axis=1).astype(jnp.int32).reshape(-1)

        xs = _scatter_rows(pos, hp, jnp.zeros((n_rows, d // 2), jnp.uint32))
        act = _moe_up(sched_up, xs, w_gate_up, b_gate_up, l)
        ys = _moe_down(sched_dn, act, w_down, b_down, l)
        xc = _combine(pos, ys, x1, gates, mod, l, row(norm_final), l == depth - 1)

    return xc.reshape(batch, s, d)
```

```python
import functools
import math

import jax
import jax.numpy as jnp
from jax import lax
from jax.experimental import pallas as pl
from jax.experimental.pallas import tpu as pltpu

F32 = jnp.float32
BF16 = jnp.bfloat16

HEAD_DIM = 128
DIFF_QK_DIM = HEAD_DIM // 2
N_MOD = 6
TOP_K = 4
SWIGLU_LIMIT = 7.0
SWIGLU_ALPHA = 1.702
ROPE_THETA = 10000.0
RMS_EPS = 1e-5

V7X_LANES = 128
V7X_VMEM_BYTES = 64 * 1024 * 1024

TM_PROJ = 512
TQ_ATTN = 256
TQ_DIFF = 512
TM_OUT = 256
TM_RANK = 256
TM_MOE = 256
TM_COMB = 256
MOE_CHUNK = 1024
MOD_CHUNK = 1024
NEG_BIG = -1e30
SB_EXIT_LOG = -100.0

_NT_DIMS = (((1,), (1,)), ((), ()))


def _params(n_axes, vmem_bytes):
    return pltpu.CompilerParams(
        dimension_semantics=("arbitrary",) * n_axes,
        vmem_limit_bytes=min(int(vmem_bytes), V7X_VMEM_BYTES - 8 * 1024 * 1024),
    )


def _rms(x):
    return x * lax.rsqrt(jnp.mean(x * x, axis=-1, keepdims=True) + RMS_EPS)


def _mod_kernel(c_ref, w_ref, b_ref, o_ref):
    c = c_ref[...]
    ca = c * (1.0 / (1.0 + jnp.exp(-c)))
    o_ref[...] = jnp.sum(ca * w_ref[...], axis=0, keepdims=True) + b_ref[...]


def _ada_mod(c_col, ada_w, ada_b):
    depth, d, n = ada_w.shape
    nc = min(MOD_CHUNK, n)
    return pl.pallas_call(
        _mod_kernel,
        grid=(depth, n // nc),
        in_specs=[
            pl.BlockSpec((d, 1), lambda l, j: (0, 0)),
            pl.BlockSpec((None, d, nc), lambda l, j: (l, 0, j)),
            pl.BlockSpec((None, 1, nc), lambda l, j: (l, 0, j)),
        ],
        out_specs=pl.BlockSpec((None, 1, nc), lambda l, j: (l, 0, j)),
        out_shape=jax.ShapeDtypeStruct((depth, 1, n), F32),
        compiler_params=_params(2, 3 * d * nc * 4 + 4 * d * V7X_LANES * 4),
        name="ada_mod",
    )(c_col, ada_w, ada_b.reshape(depth, 1, n))


def _rope_kernel(pos_ref, invf_ref, sign_ref, cos_ref, sin_ref):
    ang = pos_ref[...] * invf_ref[...]
    cos_ref[...] = jnp.cos(ang)
    sin_ref[...] = jnp.sin(ang) * sign_ref[...]


def _rope_tables(pos_col):
    s = pos_col.shape[0]
    half = DIFF_QK_DIM // 2
    inv_freq = ROPE_THETA ** (-jnp.arange(half, dtype=F32) / half)
    reps = V7X_LANES // half
    invf = jnp.tile(inv_freq, reps).reshape(1, V7X_LANES)
    sign = jnp.tile(jnp.concatenate([-jnp.ones((half,), F32), jnp.ones((half,), F32)]), reps // 2)
    ts = min(1024, s)
    tab = jax.ShapeDtypeStruct((s, V7X_LANES), F32)
    return pl.pallas_call(
        _rope_kernel,
        grid=(s // ts,),
        in_specs=[
            pl.BlockSpec((ts, 1), lambda i: (i, 0)),
            pl.BlockSpec((1, V7X_LANES), lambda i: (0, 0)),
            pl.BlockSpec((1, V7X_LANES), lambda i: (0, 0)),
        ],
        out_specs=[pl.BlockSpec((ts, V7X_LANES), lambda i: (i, 0))] * 2,
        out_shape=[tab, tab],
        compiler_params=_params(1, 32 * 1024 * 1024),
        name="rope_tables",
    )(pos_col, invf, sign.reshape(1, V7X_LANES))


def _in_proj_kernel(x_ref, nw_ref, sc_ref, sh_ref, w_ref, cos_ref, sin_ref, o_ref, h_scr, *, n_heads):
    j = pl.program_id(1)

    @pl.when(j == 0)
    def _():
        h = _rms(x_ref[...]) * nw_ref[...]
        h_scr[...] = (h * (1.0 + sc_ref[...]) + sh_ref[...]).astype(BF16)

    res = jnp.dot(h_scr[...], w_ref[...], preferred_element_type=F32)

    def store(scale, rotary):
        if rotary:
            cos = cos_ref[...]
            sin = sin_ref[...]
            lane = lax.broadcasted_iota(jnp.int32, cos.shape, 1)
            first_half = (lane % DIFF_QK_DIM) < (DIFF_QK_DIM // 2)
        for g in range(n_heads):
            blk = res[:, g * HEAD_DIM:(g + 1) * HEAD_DIM]
            if rotary:
                fwd = pltpu.roll(blk, HEAD_DIM - DIFF_QK_DIM // 2, 1)
                bwd = pltpu.roll(blk, DIFF_QK_DIM // 2, 1)
                blk = blk * cos + jnp.where(first_half, fwd, bwd) * sin
            if scale != 1.0:
                blk = blk * scale
            o_ref[g] = blk.astype(BF16)

    pl.when(j == 0)(lambda: store(DIFF_QK_DIM ** -0.5 * math.log2(math.e), True))
    pl.when(j == 1)(lambda: store(1.0, True))
    pl.when(j == 3)(lambda: store(HEAD_DIM ** -0.5, False))
    pl.when((j == 2) | (j >= 4))(lambda: store(1.0, False))


def _in_proj(x, mod, layer, norm_w, w_in, cos, sin):
    s, d = x.shape
    gw = w_in.shape[-1] // 6
    n_heads = gw // HEAD_DIM
    tm = min(TM_PROJ, s)
    vmem = 2 * tm * d * 4 + tm * d * 2 + 2 * d * gw * 2 + 2 * tm * gw * 2 + 3 * tm * gw * 4 + (4 << 20)
    return pl.pallas_call(
        functools.partial(_in_proj_kernel, n_heads=n_heads),
        grid=(s // tm, 6),
        in_specs=[
            pl.BlockSpec((tm, d), lambda i, j: (i, 0)),
            pl.BlockSpec((1, d), lambda i, j: (0, 0)),
            pl.BlockSpec((None, 1, d), lambda i, j: (layer, 0, 1)),
            pl.BlockSpec((None, 1, d), lambda i, j: (layer, 0, 0)),
            pl.BlockSpec((None, d, gw), lambda i, j: (layer, 0, j)),
            pl.BlockSpec((tm, V7X_LANES), lambda i, j: (i, 0)),
            pl.BlockSpec((tm, V7X_LANES), lambda i, j: (i, 0)),
        ],
        out_specs=pl.BlockSpec((n_heads, tm, HEAD_DIM), lambda i, j: (j, i, 0)),
        out_shape=jax.ShapeDtypeStruct((6 * n_heads, s, HEAD_DIM), BF16),
        scratch_shapes=[pltpu.VMEM((tm, d), BF16)],
        compiler_params=_params(2, vmem),
        name="in_proj",
    )(x, norm_w, mod, mod, w_in, cos, sin)


def _diff_attn_kernel(lam_ref, q_ref, k_ref, v_ref, w_ref, o_ref, m_scr, l_scr, acc_scr, *, tq, out_scale):
    qi = pl.program_id(1)
    q = q_ref[...]
    lane = lax.broadcasted_iota(jnp.int32, q.shape, 1)
    zero = jnp.zeros_like(q)
    q_sub = (jnp.where(lane < DIFF_QK_DIM, q, zero), jnp.where(lane >= DIFF_QK_DIM, q, zero))
    n_chunks = tq // V7X_LANES
    m_scr[...] = jnp.full(m_scr.shape, NEG_BIG, F32)
    l_scr[...] = jnp.zeros(l_scr.shape, F32)
    acc_scr[...] = jnp.zeros(acc_scr.shape, F32)

    def step(ki, diag):
        start = pl.multiple_of(ki * tq, tq)
        k = k_ref[pl.ds(start, tq), :]
        v = v_ref[pl.ds(start, tq), :]
        for p in range(2):
            s = lax.dot_general(q_sub[p], k, _NT_DIMS, preferred_element_type=F32)
            if diag:
                row = lax.broadcasted_iota(jnp.int32, s.shape, 0)
                col = lax.broadcasted_iota(jnp.int32, s.shape, 1)
                s = jnp.where(col <= row, s, NEG_BIG)
            chunks = [s[:, c * V7X_LANES:(c + 1) * V7X_LANES] for c in range(n_chunks)]
            m = m_scr[p]
            m_new = jnp.maximum(m, jnp.max(functools.reduce(jnp.maximum, chunks), axis=-1, keepdims=True))
            alpha = jnp.exp2(m - m_new)
            pe = [jnp.exp2(ch - m_new) for ch in chunks]
            l_scr[p] = alpha * l_scr[p] + functools.reduce(jnp.add, pe)
            pb = jnp.concatenate([x.astype(BF16) for x in pe], axis=-1)
            acc_scr[p] = alpha * acc_scr[p] + jnp.dot(pb, v, preferred_element_type=F32)
            m_scr[p] = m_new

    def body(ki, c):
        step(ki, False)
        return c

    lax.fori_loop(0, qi, body, 0)
    step(qi, True)
    den = [jnp.sum(l_scr[p], axis=-1, keepdims=True) for p in range(2)]
    a = acc_scr[0] / den[0] - lam_ref[0] * (acc_scr[1] / den[1])
    o_ref[...] = (_rms(a) * (w_ref[...] * out_scale)).astype(BF16)


def _diff_attn(qkv, lam, subln_w, out_scale):
    n_heads = qkv.shape[0] // 6
    s = qkv.shape[1]
    tq = min(TQ_DIFF, s)
    vmem = 4 * s * HEAD_DIM * 2 + 16 * tq * tq * 4 + 4 * tq * V7X_LANES * 4 + (8 << 20)
    return pl.pallas_call(
        functools.partial(_diff_attn_kernel, tq=tq, out_scale=out_scale),
        grid=(n_heads, s // tq),
        in_specs=[
            pl.BlockSpec(memory_space=pltpu.SMEM),
            pl.BlockSpec((None, tq, HEAD_DIM), lambda h, i: (h, i, 0)),
            pl.BlockSpec((None, s, HEAD_DIM), lambda h, i: (n_heads + h, 0, 0)),
            pl.BlockSpec((None, s, HEAD_DIM), lambda h, i: (2 * n_heads + h, 0, 0)),
            pl.BlockSpec((1, HEAD_DIM), lambda h, i: (0, 0)),
        ],
        out_specs=pl.BlockSpec((None, tq, HEAD_DIM), lambda h, i: (h, i, 0)),
        out_shape=jax.ShapeDtypeStruct((n_heads, s, HEAD_DIM), BF16),
        scratch_shapes=[pltpu.VMEM((2, tq, V7X_LANES), F32), pltpu.VMEM((2, tq, V7X_LANES), F32),
                        pltpu.VMEM((2, tq, HEAD_DIM), F32)],
        compiler_params=_params(2, vmem),
        name="diff_attn",
    )(lam, qkv, qkv, qkv, subln_w)


def _sb_attn_kernel(q_ref, k_ref, v_ref, w_ref, o_ref, *, tq):
    qi = pl.program_id(1)
    q = q_ref[...]
    row = lax.broadcasted_iota(jnp.int32, (tq, tq), 0)
    col = lax.broadcasted_iota(jnp.int32, (tq, tq), 1)
    strict = col < row
    after = jnp.where(row > col, 1.0, 0.0).astype(BF16)

    def block(ki, rem, acc, diag):
        start = pl.multiple_of(ki * tq, tq)
        k = k_ref[pl.ds(start, tq), :]
        v = v_ref[pl.ds(start, tq), :]
        z = lax.dot_general(q, k, _NT_DIMS, preferred_element_type=F32)
        t = jnp.log1p(jnp.exp(-jnp.abs(z)))
        log_beta = jnp.minimum(z, 0.0) - t
        log_1m = -jnp.maximum(z, 0.0) - t
        if diag:
            log_1m = jnp.where(strict, log_1m, 0.0)
        hi = log_1m.astype(BF16)
        lo = (log_1m - hi.astype(F32)).astype(BF16)
        later = (jnp.dot(hi, after, preferred_element_type=F32)
                 + jnp.dot(lo, after, preferred_element_type=F32))
        a = jnp.exp(log_beta + later + rem)
        if diag:
            a = jnp.where(strict, a, 0.0)
        acc = acc + jnp.dot(a.astype(BF16), v, preferred_element_type=F32)
        rem = rem + jnp.sum(log_1m, axis=-1, keepdims=True)
        return rem, acc

    rem, acc = block(qi, jnp.zeros((tq, 1), F32), jnp.zeros((tq, HEAD_DIM), F32), True)

    def cond(c):
        ki, rem, _ = c
        return jnp.logical_and(ki >= 0, jnp.max(rem) > SB_EXIT_LOG)

    def body(c):
        ki, rem, acc = c
        rem, acc = block(ki, rem, acc, False)
        return ki - 1, rem, acc

    _, _, acc = lax.while_loop(cond, body, (qi - 1, rem, acc))
    o_ref[...] = (_rms(acc) * w_ref[...]).astype(BF16)


def _sb_attn(qkv, norm_w):
    n_heads = qkv.shape[0] // 6
    s = qkv.shape[1]
    tq = min(TQ_ATTN, s)
    vmem = 4 * s * HEAD_DIM * 2 + 16 * tq * tq * 4 + (8 << 20)
    return pl.pallas_call(
        functools.partial(_sb_attn_kernel, tq=tq),
        grid=(n_heads, s // tq),
        in_specs=[
            pl.BlockSpec((None, tq, HEAD_DIM), lambda h, i: (3 * n_heads + h, i, 0)),
            pl.BlockSpec((None, s, HEAD_DIM), lambda h, i: (4 * n_heads + h, 0, 0)),
            pl.BlockSpec((None, s, HEAD_DIM), lambda h, i: (5 * n_heads + h, 0, 0)),
            pl.BlockSpec((1, HEAD_DIM), lambda h, i: (0, 0)),
        ],
        out_specs=pl.BlockSpec((None, tq, HEAD_DIM), lambda h, i: (h, i, 0)),
        out_shape=jax.ShapeDtypeStruct((n_heads, s, HEAD_DIM), BF16),
        compiler_params=_params(2, vmem),
        name="sb_attn",
    )(qkv, qkv, qkv, norm_w)


def _out_router_kernel(a_ref, b_ref, w_ref, x_ref, ga_ref, nw_ref, sc_ref, sh_ref, rwh_ref, rwl_ref, rb_ref,
                       x1_ref, hp_ref, eidx_ref, gate_ref, sel_ref, *, n_heads):
    mix = jnp.concatenate([a_ref[g] for g in range(n_heads)] + [b_ref[g] for g in range(n_heads)], axis=-1)
    y = jnp.dot(mix, w_ref[...], preferred_element_type=F32)
    x1 = x_ref[...] + ga_ref[...] * y
    x1_ref[...] = x1
    h = _rms(x1) * nw_ref[...]
    h = h * (1.0 + sc_ref[...]) + sh_ref[...]
    half = h.shape[-1] // 2
    hp_ref[...] = pltpu.pack_elementwise([h[:, :half], h[:, half:]], packed_dtype=BF16)

    h_hi = h.astype(BF16)
    h_lo = (h - h_hi.astype(F32)).astype(BF16)
    logits = (jnp.dot(h_hi, rwh_ref[...], preferred_element_type=F32)
              + jnp.dot(h_lo, rwh_ref[...], preferred_element_type=F32)
              + jnp.dot(h_hi, rwl_ref[...], preferred_element_type=F32)) + rb_ref[...]

    lane = lax.broadcasted_iota(jnp.int32, logits.shape, 1).astype(F32)
    vals, idxs = [], []
    sel = jnp.zeros(logits.shape, F32)
    for _ in range(TOP_K):
        m = jnp.max(logits, axis=-1, keepdims=True)
        idx = jnp.min(jnp.where(logits == m, lane, float(V7X_LANES)), axis=-1, keepdims=True)
        hit = lane == idx
        vals.append(m)
        idxs.append(idx)
        sel = jnp.where(hit, 1.0, sel)
        logits = jnp.where(hit, -3e38, logits)
    exps = [jnp.exp(v - vals[0]) for v in vals]
    denom = exps[0] + exps[1] + exps[2] + exps[3]
    eidx = jnp.zeros(logits.shape, F32)
    gate = jnp.zeros(logits.shape, F32)
    for kk in range(TOP_K):
        eidx = jnp.where(lane == float(kk), idxs[kk], eidx)
        gate = jnp.where(lane == float(kk), exps[kk] / denom, gate)
    eidx_ref[...] = eidx.astype(jnp.int32)
    gate_ref[...] = gate
    sel_ref[...] = sel.astype(BF16)


def _out_router(a_mix, b_mix, w_out, x, mod, layer, norm_w, rw_hi, rw_lo, rb):
    n_heads, s, _ = a_mix.shape
    d = x.shape[1]
    tm = min(TM_OUT, s)
    vmem = 2 * d * d * 2 + 10 * tm * d * 4 + (8 << 20)
    row = lambda i: (i, 0)
    fixed = lambda i: (0, 0)
    lanes = jax.ShapeDtypeStruct((s, V7X_LANES), F32)
    return pl.pallas_call(
        functools.partial(_out_router_kernel, n_heads=n_heads),
        grid=(s // tm,),
        in_specs=[
            pl.BlockSpec((n_heads, tm, HEAD_DIM), lambda i: (0, i, 0)),
            pl.BlockSpec((n_heads, tm, HEAD_DIM), lambda i: (0, i, 0)),
            pl.BlockSpec((None, d, d), lambda i: (layer, 0, 0)),
            pl.BlockSpec((tm, d), row),
            pl.BlockSpec((None, 1, d), lambda i: (layer, 0, 2)),
            pl.BlockSpec((1, d), fixed),
            pl.BlockSpec((None, 1, d), lambda i: (layer, 0, 4)),
            pl.BlockSpec((None, 1, d), lambda i: (layer, 0, 3)),
            pl.BlockSpec((d, V7X_LANES), fixed),
            pl.BlockSpec((d, V7X_LANES), fixed),
            pl.BlockSpec((1, V7X_LANES), fixed),
        ],
        out_specs=[
            pl.BlockSpec((tm, d), row),
            pl.BlockSpec((tm, d // 2), row),
            pl.BlockSpec((tm, V7X_LANES), row),
            pl.BlockSpec((tm, V7X_LANES), row),
            pl.BlockSpec((tm, V7X_LANES), row),
        ],
        out_shape=[
            jax.ShapeDtypeStruct((s, d), F32),
            jax.ShapeDtypeStruct((s, d // 2), jnp.uint32),
            jax.ShapeDtypeStruct((s, V7X_LANES), jnp.int32),
            lanes,
            jax.ShapeDtypeStruct((s, V7X_LANES), BF16),
        ],
        compiler_params=_params(1, vmem),
        name="out_router",
    )(a_mix, b_mix, w_out, x, mod, norm_w, mod, mod, rw_hi, rw_lo, rb)


def _rank_kernel(sel_ref, rank_ref, count_ref, run_scr):
    i = pl.program_id(0)

    @pl.when(i == 0)
    def _():
        run_scr[...] = jnp.zeros_like(run_scr)

    sel = sel_ref[...]
    tm = sel.shape[0]
    row = lax.broadcasted_iota(jnp.int32, (tm, tm), 0)
    col = lax.broadcasted_iota(jnp.int32, (tm, tm), 1)
    before = jnp.where(col < row, 1.0, 0.0).astype(BF16)
    run = run_scr[...]
    rank_ref[...] = jnp.dot(before, sel, preferred_element_type=F32) + run
    run = run + jnp.sum(sel.astype(F32), axis=0, keepdims=True)
    run_scr[...] = run
    count_ref[...] = run


def _rank(sel):
    s = sel.shape[0]
    tm = min(TM_RANK, s)
    return pl.pallas_call(
        _rank_kernel,
        grid=(s // tm,),
        in_specs=[pl.BlockSpec((tm, V7X_LANES), lambda i: (i, 0))],
        out_specs=[pl.BlockSpec((tm, V7X_LANES), lambda i: (i, 0)),
                   pl.BlockSpec((1, V7X_LANES), lambda i: (0, 0))],
        out_shape=[jax.ShapeDtypeStruct((s, V7X_LANES), F32), jax.ShapeDtypeStruct((1, V7X_LANES), F32)],
        scratch_shapes=[pltpu.VMEM((1, V7X_LANES), F32)],
        compiler_params=_params(1, 16 * 1024 * 1024),
        name="route_rank",
    )(sel)


def _scatter_kernel(pos_ref, h_ref, xs_in_ref, xs_ref, sem, *, tm):
    del xs_in_ref

    def issue(r, _):
        for kk in range(TOP_K):
            dst = pos_ref[r * TOP_K + kk]
            pltpu.make_async_copy(h_ref.at[pl.ds(r, 1)], xs_ref.at[pl.ds(dst, 1)], sem).start()
        return ()

    lax.fori_loop(0, tm, issue, ())
    for _ in range(TOP_K):
        pltpu.make_async_copy(h_ref, xs_ref.at[pl.ds(0, tm)], sem).wait()


def _scatter_rows(pos_flat, hp, xs_zero):
    s, w = hp.shape
    tm = min(TM_COMB, s)
    return pl.pallas_call(
        functools.partial(_scatter_kernel, tm=tm),
        grid=(s // tm,),
        in_specs=[
            pl.BlockSpec((tm * TOP_K,), lambda i: (i,), memory_space=pltpu.SMEM),
            pl.BlockSpec((tm, w), lambda i: (i, 0)),
            pl.BlockSpec(memory_space=pl.ANY),
        ],
        out_specs=pl.BlockSpec(memory_space=pl.ANY),
        out_shape=jax.ShapeDtypeStruct(xs_zero.shape, xs_zero.dtype),
        scratch_shapes=[pltpu.SemaphoreType.DMA(())],
        input_output_aliases={2: 0},
        compiler_params=_params(1, 16 * 1024 * 1024),
        name="moe_scatter",
    )(pos_flat, hp, xs_zero)


def _moe_up_kernel(tile_ref, oc_ref, wc_ref, exp_ref, first_ref, valid_ref, x_ref, w_ref, b_ref, o_ref, wb_scr):
    del tile_ref, oc_ref, wc_ref, exp_ref
    s = pl.program_id(0)

    @pl.when(first_ref[s] == 1)
    def _():
        wb_scr[...] = w_ref[...].astype(BF16)

    @pl.when(valid_ref[s] == 1)
    def _():
        xp = x_ref[...]
        half = xp.shape[-1]
        x_lo = pltpu.unpack_elementwise(xp, index=0, packed_dtype=BF16, unpacked_dtype=F32).astype(BF16)
        x_hi = pltpu.unpack_elementwise(xp, index=1, packed_dtype=BF16, unpacked_dtype=F32).astype(BF16)
        gu = (jnp.dot(x_lo, wb_scr[0:half, :], preferred_element_type=F32)
              + jnp.dot(x_hi, wb_scr[half:2 * half, :], preferred_element_type=F32)) + b_ref[...]
        gate = jnp.minimum(gu, SWIGLU_LIMIT)
        glu = gate * (1.0 / (1.0 + jnp.exp(-SWIGLU_ALPHA * gate)))
        up1 = jnp.clip(gu, -SWIGLU_LIMIT, SWIGLU_LIMIT) + 1.0
        pair = 2 * V7X_LANES
        r = lax.broadcasted_iota(jnp.int32, (pair, V7X_LANES), 0)
        c = lax.broadcasted_iota(jnp.int32, (pair, V7X_LANES), 1)
        even = jnp.where(r == 2 * c, 1.0, 0.0).astype(BF16)
        outs = []
        for q in range(gu.shape[-1] // pair):
            sl = slice(q * pair, (q + 1) * pair)
            prod = glu[:, sl] * pltpu.roll(up1[:, sl], pair - 1, 1)
            outs.append(jnp.dot(prod.astype(BF16), even, preferred_element_type=F32))
        o_ref[...] = jnp.concatenate(outs, axis=-1).astype(BF16)

    @pl.when(valid_ref[s] == 0)
    def _():
        o_ref[...] = jnp.zeros_like(o_ref)


def _moe_up(sched, xs, w_gate_up, b_gate_up, layer):
    p, half = xs.shape
    n_exp, d, f2 = w_gate_up.shape[1:]
    nc = min(MOE_CHUNK, f2)
    tm = TM_MOE
    n_steps = sched[0].shape[0]
    vmem = 2 * d * nc * 4 + d * nc * 2 + 2 * tm * half * 4 + 8 * tm * nc * 4 + (6 << 20)
    grid_spec = pltpu.PrefetchScalarGridSpec(
        num_scalar_prefetch=6,
        grid=(n_steps,),
        in_specs=[
            pl.BlockSpec((tm, half), lambda s, t, oc, wc, e, f, v: (t[s], 0)),
            pl.BlockSpec((None, None, d, nc), lambda s, t, oc, wc, e, f, v: (layer, e[s], 0, wc[s])),
            pl.BlockSpec((None, None, 1, nc), lambda s, t, oc, wc, e, f, v: (layer, e[s], 0, wc[s])),
        ],
        out_specs=pl.BlockSpec((tm, nc // 2), lambda s, t, oc, wc, e, f, v: (t[s], oc[s])),
        scratch_shapes=[pltpu.VMEM((d, nc), BF16)],
    )
    return pl.pallas_call(
        _moe_up_kernel,
        grid_spec=grid_spec,
        out_shape=jax.ShapeDtypeStruct((p, f2 // 2), BF16),
        compiler_params=_params(1, vmem),
        name="moe_up",
    )(*sched, xs, w_gate_up, b_gate_up.reshape(b_gate_up.shape[0], n_exp, 1, f2))


def _moe_down_kernel(tile_ref, oc_ref, wc_ref, exp_ref, first_ref, valid_ref, a_ref, w_ref, b_ref, o_ref, wb_scr):
    del tile_ref, oc_ref, wc_ref, exp_ref
    s = pl.program_id(0)

    @pl.when(first_ref[s] == 1)
    def _():
        wb_scr[...] = w_ref[...].astype(BF16)

    @pl.when(valid_ref[s] == 1)
    def _():
        o_ref[...] = jnp.dot(a_ref[...], wb_scr[...], preferred_element_type=F32) + b_ref[...]

    @pl.when(valid_ref[s] == 0)
    def _():
        o_ref[...] = jnp.zeros_like(o_ref)


def _moe_down(sched, act, w_down, b_down, layer):
    p, f = act.shape
    n_exp, _, d = w_down.shape[1:]
    nc = min(MOE_CHUNK, d)
    tm = TM_MOE
    n_steps = sched[0].shape[0]
    vmem = 2 * f * nc * 4 + f * nc * 2 + 2 * tm * f * 2 + 4 * tm * nc * 4 + (6 << 20)
    grid_spec = pltpu.PrefetchScalarGridSpec(
        num_scalar_prefetch=6,
        grid=(n_steps,),
        in_specs=[
            pl.BlockSpec((tm, f), lambda s, t, oc, wc, e, fi, v: (t[s], 0)),
            pl.BlockSpec((None, None, f, nc), lambda s, t, oc, wc, e, fi, v: (layer, e[s], 0, wc[s])),
            pl.BlockSpec((None, None, 1, nc), lambda s, t, oc, wc, e, fi, v: (layer, e[s], 0, wc[s])),
        ],
        out_specs=pl.BlockSpec((tm, nc), lambda s, t, oc, wc, e, fi, v: (t[s], oc[s])),
        scratch_shapes=[pltpu.VMEM((f, nc), BF16)],
    )
    return pl.pallas_call(
        _moe_down_kernel,
        grid_spec=grid_spec,
        out_shape=jax.ShapeDtypeStruct((p, d), F32),
        compiler_params=_params(1, vmem),
        name="moe_down",
    )(*sched, act, w_down, b_down.reshape(b_down.shape[0], n_exp, 1, d))


def _moe_schedule(counts, n_tiles, n_chunks):
    tm = TM_MOE
    n_exp = counts.shape[0]
    tiles_per = (counts + tm - 1) // tm
    tile_end = jnp.cumsum(tiles_per)
    tile_start = tile_end - tiles_per
    total = tile_end[-1]
    step = jnp.arange(n_tiles * n_chunks, dtype=jnp.int32)
    lin = step // n_chunks
    valid = lin < total
    owner = lambda t: jnp.minimum(jnp.sum((tile_end[None, :] <= t[:, None]).astype(jnp.int32), axis=1), n_exp - 1)
    expert = owner(lin)
    rel = step - tile_start[expert] * n_chunks
    per = jnp.maximum(tiles_per[expert], 1)
    chunk = rel // per
    tile = tile_start[expert] + rel % per
    first = (rel % per) == 0
    last_expert = owner((total - 1).reshape(1))[0]
    tile = jnp.where(valid, tile, lin)
    out_chunk = jnp.where(valid, chunk, step % n_chunks)
    w_chunk = jnp.where(valid, chunk, n_chunks - 1)
    expert = jnp.where(valid, expert, last_expert)
    first = jnp.where(valid, first, False)
    i32 = lambda a: a.astype(jnp.int32)
    return (i32(tile), i32(out_chunk), i32(w_chunk), i32(expert), i32(first), i32(valid)), tile_start * tm


def _combine_kernel(pos_ref, ys_ref, x_ref, gate_ref, gf_ref, nw_ref, o_ref, buf, sem, *, tm, final):
    def issue(r, _):
        for kk in range(TOP_K):
            src = pos_ref[r * TOP_K + kk]
            pltpu.make_async_copy(ys_ref.at[pl.ds(src, 1)], buf.at[kk, pl.ds(r, 1)], sem).start()
        return ()

    lax.fori_loop(0, tm, issue, ())
    for kk in range(TOP_K):
        pltpu.make_async_copy(ys_ref.at[pl.ds(0, tm)], buf.at[kk], sem).wait()

    gates = gate_ref[...]
    y = gates[:, 0:1] * buf[0]
    for kk in range(1, TOP_K):
        y = y + gates[:, kk:kk + 1] * buf[kk]
    out = x_ref[...] + gf_ref[...] * y
    if final:
        out = _rms(out) * nw_ref[...]
    o_ref[...] = out


def _combine(pos_flat, ys, x1, gates, mod, layer, norm_final, final):
    s, d = x1.shape
    tm = min(TM_COMB, s)
    vmem = TOP_K * tm * d * 4 + 6 * tm * d * 4 + (6 << 20)
    return pl.pallas_call(
        functools.partial(_combine_kernel, tm=tm, final=final),
        grid=(s // tm,),
        in_specs=[
            pl.BlockSpec((tm * TOP_K,), lambda i: (i,), memory_space=pltpu.SMEM),
            pl.BlockSpec(memory_space=pl.ANY),
            pl.BlockSpec((tm, d), lambda i: (i, 0)),
            pl.BlockSpec((tm, V7X_LANES), lambda i: (i, 0)),
            pl.BlockSpec((None, 1, d), lambda i: (layer, 0, 5)),
            pl.BlockSpec((1, d), lambda i: (0, 0)),
        ],
        out_specs=pl.BlockSpec((tm, d), lambda i: (i, 0)),
        out_shape=jax.ShapeDtypeStruct((s, d), F32),
        scratch_shapes=[pltpu.VMEM((TOP_K, tm, d), F32), pltpu.SemaphoreType.DMA(())],
        compiler_params=_params(1, vmem),
        name="moe_combine",
    )(pos_flat, ys, x1, gates, mod, norm_final)


def kernel(x, c, positions, ada_w, ada_b, norm_mix, w_in, w_out, lambda_q1, lambda_k1, lambda_q2, lambda_k2,
           subln_w, sb_norm_w, norm_ffn, router_w, router_b, w_gate_up, b_gate_up, w_down, b_down, norm_final):
    batch, s, d = x.shape
    assert batch == 1, "the kernels are written for one sequence"
    depth = ada_w.shape[0]
    n_exp = router_w.shape[-1]
    assert n_exp <= V7X_LANES

    xc = x.reshape(s, d)
    mod = _ada_mod(c.reshape(d, 1), ada_w, ada_b)
    cos, sin = _rope_tables(positions.reshape(s, 1).astype(F32))
    w_in_b = w_in.astype(BF16)
    w_out_b = w_out.astype(BF16)

    n_tiles = (s * TOP_K) // TM_MOE + n_exp
    n_rows = n_tiles * TM_MOE
    pad = V7X_LANES - n_exp
    row = lambda a: a.reshape(1, -1)

    for l in range(depth):
        lam_init = 0.8 - 0.6 * math.exp(-0.3 * l)
        lam = (jnp.exp(jnp.sum(lambda_q1[l] * lambda_k1[l])) - jnp.exp(jnp.sum(lambda_q2[l] * lambda_k2[l]))
               + lam_init).reshape(1).astype(F32)

        qkv = _in_proj(xc, mod, l, row(norm_mix[l]), w_in_b, cos, sin)
        a_mix = _diff_attn(qkv, lam, row(subln_w[l]), 1.0 - lam_init)
        b_mix = _sb_attn(qkv, row(sb_norm_w[l]))

        rw = jnp.pad(router_w[l], ((0, 0), (0, pad)))
        rw_hi = rw.astype(BF16)
        rw_lo = (rw - rw_hi.astype(F32)).astype(BF16)
        rb = jnp.pad(router_b[l], (0, pad), constant_values=NEG_BIG).reshape(1, V7X_LANES)
        x1, hp, eidx, gates, sel = _out_router(a_mix, b_mix, w_out_b, xc, mod, l, row(norm_ffn[l]),
                                               rw_hi, rw_lo, rb)

        rank, counts = _rank(sel)
        counts = counts[0, :n_exp].astype(jnp.int32)
        n_chunks_up = w_gate_up.shape[-1] // min(MOE_CHUNK, w_gate_up.shape[-1])
        n_chunks_dn = w_down.shape[-1] // min(MOE_CHUNK, w_down.shape[-1])
        sched_up, row_start = _moe_schedule(counts, n_tiles, n_chunks_up)
        sched_dn, _ = _moe_schedule(counts, n_tiles, n_chunks_dn)
        slot = rank[:, :n_exp] + row_start.astype(F32)[None, :]
        pos = jnp.take_along_axis(slot, eidx[:, :TOP_K], axis=1).astype(jnp.int32).reshape(-1)

        xs = _scatter_rows(pos, hp, jnp.zeros((n_rows, d // 2), jnp.uint32))
        act = _moe_up(sched_up, xs, w_gate_up, b_gate_up, l)
        ys = _moe_down(sched_dn, act, w_down, b_down, l)
        xc = _combine(pos, ys, x1, gates, mod, l, row(norm_final), l == depth - 1)

    return xc.reshape(batch, s, d)
```

```python
import functools
import math

import jax
import jax.numpy as jnp
from jax import lax
from jax.experimental import pallas as pl
from jax.experimental.pallas import tpu as pltpu

F32 = jnp.float32
BF16 = jnp.bfloat16

HEAD_DIM = 128
DIFF_QK_DIM = HEAD_DIM // 2
N_MOD = 6
TOP_K = 4
SWIGLU_LIMIT = 7.0
SWIGLU_ALPHA = 1.702
ROPE_THETA = 10000.0
RMS_EPS = 1e-5

V7X_LANES = 128
V7X_VMEM_BYTES = 64 * 1024 * 1024

TM_PROJ = 512
TQ_ATTN = 256
TQ_DIFF = 512
SB_HEADS_PER_STEP = 4
TM_OUT = 256
TM_RANK = 256
TM_MOE = 256
TM_COMB = 256
MOE_CHUNK = 1024
MOD_CHUNK = 1024
NEG_BIG = -1e30
SB_EXIT_LOG = -100.0

_NT_DIMS = (((1,), (1,)), ((), ()))


def _params(n_axes, vmem_bytes):
    return pltpu.CompilerParams(
        dimension_semantics=("arbitrary",) * n_axes,
        vmem_limit_bytes=min(int(vmem_bytes), V7X_VMEM_BYTES - 8 * 1024 * 1024),
    )


def _rms(x):
    return x * lax.rsqrt(jnp.mean(x * x, axis=-1, keepdims=True) + RMS_EPS)


def _mod_kernel(c_ref, w_ref, b_ref, o_ref):
    c = c_ref[...]
    ca = c * (1.0 / (1.0 + jnp.exp(-c)))
    o_ref[...] = jnp.sum(ca * w_ref[...], axis=0, keepdims=True) + b_ref[...]


def _ada_mod(c_col, ada_w, ada_b):
    depth, d, n = ada_w.shape
    nc = min(MOD_CHUNK, n)
    return pl.pallas_call(
        _mod_kernel,
        grid=(depth, n // nc),
        in_specs=[
            pl.BlockSpec((d, 1), lambda l, j: (0, 0)),
            pl.BlockSpec((None, d, nc), lambda l, j: (l, 0, j)),
            pl.BlockSpec((None, 1, nc), lambda l, j: (l, 0, j)),
        ],
        out_specs=pl.BlockSpec((None, 1, nc), lambda l, j: (l, 0, j)),
        out_shape=jax.ShapeDtypeStruct((depth, 1, n), F32),
        compiler_params=_params(2, 3 * d * nc * 4 + 4 * d * V7X_LANES * 4),
        name="ada_mod",
    )(c_col, ada_w, ada_b.reshape(depth, 1, n))


def _rope_kernel(pos_ref, invf_ref, sign_ref, cos_ref, sin_ref):
    ang = pos_ref[...] * invf_ref[...]
    cos_ref[...] = jnp.cos(ang)
    sin_ref[...] = jnp.sin(ang) * sign_ref[...]


def _rope_tables(pos_col):
    s = pos_col.shape[0]
    half = DIFF_QK_DIM // 2
    inv_freq = ROPE_THETA ** (-jnp.arange(half, dtype=F32) / half)
    reps = V7X_LANES // half
    invf = jnp.tile(inv_freq, reps).reshape(1, V7X_LANES)
    sign = jnp.tile(jnp.concatenate([-jnp.ones((half,), F32), jnp.ones((half,), F32)]), reps // 2)
    ts = min(1024, s)
    tab = jax.ShapeDtypeStruct((s, V7X_LANES), F32)
    return pl.pallas_call(
        _rope_kernel,
        grid=(s // ts,),
        in_specs=[
            pl.BlockSpec((ts, 1), lambda i: (i, 0)),
            pl.BlockSpec((1, V7X_LANES), lambda i: (0, 0)),
            pl.BlockSpec((1, V7X_LANES), lambda i: (0, 0)),
        ],
        out_specs=[pl.BlockSpec((ts, V7X_LANES), lambda i: (i, 0))] * 2,
        out_shape=[tab, tab],
        compiler_params=_params(1, 32 * 1024 * 1024),
        name="rope_tables",
    )(pos_col, invf, sign.reshape(1, V7X_LANES))


def _in_proj_kernel(x_ref, nw_ref, sc_ref, sh_ref, w_ref, cos_ref, sin_ref, o_ref, h_scr, *, n_heads):
    j = pl.program_id(1)

    @pl.when(j == 0)
    def _():
        h = _rms(x_ref[...]) * nw_ref[...]
        h_scr[...] = (h * (1.0 + sc_ref[...]) + sh_ref[...]).astype(BF16)

    res = jnp.dot(h_scr[...], w_ref[...], preferred_element_type=F32)

    def store(scale, rotary):
        if rotary:
            cos = cos_ref[...]
            sin = sin_ref[...]
            lane = lax.broadcasted_iota(jnp.int32, cos.shape, 1)
            first_half = (lane % DIFF_QK_DIM) < (DIFF_QK_DIM // 2)
        for g in range(n_heads):
            blk = res[:, g * HEAD_DIM:(g + 1) * HEAD_DIM]
            if rotary:
                fwd = pltpu.roll(blk, HEAD_DIM - DIFF_QK_DIM // 2, 1)
                bwd = pltpu.roll(blk, DIFF_QK_DIM // 2, 1)
                blk = blk * cos + jnp.where(first_half, fwd, bwd) * sin
            if scale != 1.0:
                blk = blk * scale
            o_ref[g] = blk.astype(BF16)

    pl.when(j == 0)(lambda: store(DIFF_QK_DIM ** -0.5 * math.log2(math.e), True))
    pl.when(j == 1)(lambda: store(1.0, True))
    pl.when(j == 3)(lambda: store(HEAD_DIM ** -0.5, False))
    pl.when((j == 2) | (j >= 4))(lambda: store(1.0, False))


def _in_proj(x, mod, layer, norm_w, w_in, cos, sin):
    s, d = x.shape
    gw = w_in.shape[-1] // 6
    n_heads = gw // HEAD_DIM
    tm = min(TM_PROJ, s)
    vmem = 2 * tm * d * 4 + tm * d * 2 + 2 * d * gw * 2 + 2 * tm * gw * 2 + 3 * tm * gw * 4 + (4 << 20)
    return pl.pallas_call(
        functools.partial(_in_proj_kernel, n_heads=n_heads),
        grid=(s // tm, 6),
        in_specs=[
            pl.BlockSpec((tm, d), lambda i, j: (i, 0)),
            pl.BlockSpec((1, d), lambda i, j: (0, 0)),
            pl.BlockSpec((None, 1, d), lambda i, j: (layer, 0, 1)),
            pl.BlockSpec((None, 1, d), lambda i, j: (layer, 0, 0)),
            pl.BlockSpec((None, d, gw), lambda i, j: (layer, 0, j)),
            pl.BlockSpec((tm, V7X_LANES), lambda i, j: (i, 0)),
            pl.BlockSpec((tm, V7X_LANES), lambda i, j: (i, 0)),
        ],
        out_specs=pl.BlockSpec((n_heads, tm, HEAD_DIM), lambda i, j: (j, i, 0)),
        out_shape=jax.ShapeDtypeStruct((6 * n_heads, s, HEAD_DIM), BF16),
        scratch_shapes=[pltpu.VMEM((tm, d), BF16)],
        compiler_params=_params(2, vmem),
        name="in_proj",
    )(x, norm_w, mod, mod, w_in, cos, sin)


def _diff_attn_kernel(lam_ref, q_ref, k_ref, v_ref, w_ref, o_ref, m_scr, l_scr, acc_scr, *, tq, out_scale):
    qi = pl.program_id(1)
    q = q_ref[...]
    lane = lax.broadcasted_iota(jnp.int32, q.shape, 1)
    zero = jnp.zeros_like(q)
    q_sub = (jnp.where(lane < DIFF_QK_DIM, q, zero), jnp.where(lane >= DIFF_QK_DIM, q, zero))
    n_chunks = tq // V7X_LANES
    m_scr[...] = jnp.full(m_scr.shape, NEG_BIG, F32)
    l_scr[...] = jnp.zeros(l_scr.shape, F32)
    acc_scr[...] = jnp.zeros(acc_scr.shape, F32)

    def step(ki, diag):
        start = pl.multiple_of(ki * tq, tq)
        k = k_ref[pl.ds(start, tq), :]
        v = v_ref[pl.ds(start, tq), :]
        for p in range(2):
            s = lax.dot_general(q_sub[p], k, _NT_DIMS, preferred_element_type=F32)
            if diag:
                row = lax.broadcasted_iota(jnp.int32, s.shape, 0)
                col = lax.broadcasted_iota(jnp.int32, s.shape, 1)
                s = jnp.where(col <= row, s, NEG_BIG)
            chunks = [s[:, c * V7X_LANES:(c + 1) * V7X_LANES] for c in range(n_chunks)]
            m = m_scr[p]
            m_new = jnp.maximum(m, jnp.max(functools.reduce(jnp.maximum, chunks), axis=-1, keepdims=True))
            alpha = jnp.exp2(m - m_new)
            pe = [jnp.exp2(ch - m_new) for ch in chunks]
            l_scr[p] = alpha * l_scr[p] + functools.reduce(jnp.add, pe)
            pb = jnp.concatenate([x.astype(BF16) for x in pe], axis=-1)
            acc_scr[p] = alpha * acc_scr[p] + jnp.dot(pb, v, preferred_element_type=F32)
            m_scr[p] = m_new

    def body(ki, c):
        step(ki, False)
        return c

    lax.fori_loop(0, qi, body, 0)
    step(qi, True)
    den = [jnp.sum(l_scr[p], axis=-1, keepdims=True) for p in range(2)]
    a = acc_scr[0] / den[0] - lam_ref[0] * (acc_scr[1] / den[1])
    o_ref[...] = (_rms(a) * (w_ref[...] * out_scale)).astype(BF16)


def _diff_attn(qkv, lam, subln_w, out_scale):
    n_heads = qkv.shape[0] // 6
    s = qkv.shape[1]
    tq = min(TQ_DIFF, s)
    vmem = 4 * s * HEAD_DIM * 2 + 16 * tq * tq * 4 + 4 * tq * V7X_LANES * 4 + (8 << 20)
    return pl.pallas_call(
        functools.partial(_diff_attn_kernel, tq=tq, out_scale=out_scale),
        grid=(n_heads, s // tq),
        in_specs=[
            pl.BlockSpec(memory_space=pltpu.SMEM),
            pl.BlockSpec((None, tq, HEAD_DIM), lambda h, i: (h, i, 0)),
            pl.BlockSpec((None, s, HEAD_DIM), lambda h, i: (n_heads + h, 0, 0)),
            pl.BlockSpec((None, s, HEAD_DIM), lambda h, i: (2 * n_heads + h, 0, 0)),
            pl.BlockSpec((1, HEAD_DIM), lambda h, i: (0, 0)),
        ],
        out_specs=pl.BlockSpec((None, tq, HEAD_DIM), lambda h, i: (h, i, 0)),
        out_shape=jax.ShapeDtypeStruct((n_heads, s, HEAD_DIM), BF16),
        scratch_shapes=[pltpu.VMEM((2, tq, V7X_LANES), F32), pltpu.VMEM((2, tq, V7X_LANES), F32),
                        pltpu.VMEM((2, tq, HEAD_DIM), F32)],
        compiler_params=_params(2, vmem),
        name="diff_attn",
    )(lam, qkv, qkv, qkv, subln_w)


def _sb_attn_kernel(q_ref, k_ref, v_ref, w_ref, o_ref, *, tq, n_group):
    qi = pl.program_id(1)
    n_chunks = tq // V7X_LANES
    row = lax.broadcasted_iota(jnp.int32, (tq, tq), 0)
    col = lax.broadcasted_iota(jnp.int32, (tq, tq), 1)
    strict = col < row
    after = jnp.where(row > col, 1.0, 0.0).astype(BF16)
    ones = jnp.ones((tq, V7X_LANES), BF16)

    def block(h, ki, rem, acc, diag):
        start = pl.multiple_of(ki * tq, tq)
        k = k_ref[h, pl.ds(start, tq), :]
        v = v_ref[h, pl.ds(start, tq), :]
        z = lax.dot_general(q_ref[h], k, _NT_DIMS, preferred_element_type=F32)
        t = jnp.log1p(jnp.exp(-jnp.abs(z)))
        log_beta = jnp.minimum(z, 0.0) - t
        log_1m = -jnp.maximum(z, 0.0) - t
        if diag:
            log_1m = jnp.where(strict, log_1m, 0.0)
        hi = log_1m.astype(BF16)
        lo = (log_1m - hi.astype(F32)).astype(BF16)
        later = (jnp.dot(hi, after, preferred_element_type=F32)
                 + jnp.dot(lo, after, preferred_element_type=F32))
        total = (jnp.dot(hi, ones, preferred_element_type=F32)
                 + jnp.dot(lo, ones, preferred_element_type=F32))
        log_a = log_beta + later
        a = []
        for c in range(n_chunks):
            sl = slice(c * V7X_LANES, (c + 1) * V7X_LANES)
            a_c = jnp.exp(log_a[:, sl] + rem)
            if diag:
                a_c = jnp.where(strict[:, sl], a_c, 0.0)
            a.append(a_c.astype(BF16))
        acc = acc + jnp.dot(jnp.concatenate(a, axis=-1), v, preferred_element_type=F32)
        return rem + total, acc

    zero = jnp.zeros((tq, V7X_LANES), F32)
    state = [block(h, qi, zero, zero, True) for h in range(n_group)]
    rems = tuple(s[0] for s in state)
    accs = tuple(s[1] for s in state)

    def cond(c):
        ki, rems, _ = c
        return jnp.logical_and(ki >= 0, jnp.max(functools.reduce(jnp.maximum, rems)) > SB_EXIT_LOG)

    def body(c):
        ki, rems, accs = c
        state = [block(h, ki, rems[h], accs[h], False) for h in range(n_group)]
        return ki - 1, tuple(s[0] for s in state), tuple(s[1] for s in state)

    _, _, accs = lax.while_loop(cond, body, (qi - 1, rems, accs))
    for h in range(n_group):
        o_ref[h] = (_rms(accs[h]) * w_ref[...]).astype(BF16)


def _sb_attn(qkv, norm_w):
    n_heads = qkv.shape[0] // 6
    s = qkv.shape[1]
    tq = min(TQ_ATTN, s)
    n_group = math.gcd(SB_HEADS_PER_STEP, n_heads)
    vmem = 4 * n_group * s * HEAD_DIM * 2 + n_group * 12 * tq * tq * 4 + (8 << 20)
    blk = lambda base: (lambda hg, i: (base * n_heads // n_group + hg, 0, 0))
    return pl.pallas_call(
        functools.partial(_sb_attn_kernel, tq=tq, n_group=n_group),
        grid=(n_heads // n_group, s // tq),
        in_specs=[
            pl.BlockSpec((n_group, tq, HEAD_DIM), lambda hg, i: (3 * n_heads // n_group + hg, i, 0)),
            pl.BlockSpec((n_group, s, HEAD_DIM), blk(4)),
            pl.BlockSpec((n_group, s, HEAD_DIM), blk(5)),
            pl.BlockSpec((1, HEAD_DIM), lambda hg, i: (0, 0)),
        ],
        out_specs=pl.BlockSpec((n_group, tq, HEAD_DIM), lambda hg, i: (hg, i, 0)),
        out_shape=jax.ShapeDtypeStruct((n_heads, s, HEAD_DIM), BF16),
        compiler_params=_params(2, vmem),
        name="sb_attn",
    )(qkv, qkv, qkv, norm_w)


def _out_router_kernel(a_ref, b_ref, w_ref, x_ref, ga_ref, nw_ref, sc_ref, sh_ref, rwh_ref, rwl_ref, rb_ref,
                       x1_ref, hp_ref, eidx_ref, gate_ref, sel_ref, *, n_heads):
    mix = jnp.concatenate([a_ref[g] for g in range(n_heads)] + [b_ref[g] for g in range(n_heads)], axis=-1)
    y = jnp.dot(mix, w_ref[...], preferred_element_type=F32)
    x1 = x_ref[...] + ga_ref[...] * y
    x1_ref[...] = x1
    h = _rms(x1) * nw_ref[...]
    h = h * (1.0 + sc_ref[...]) + sh_ref[...]
    half = h.shape[-1] // 2
    hp_ref[...] = pltpu.pack_elementwise([h[:, :half], h[:, half:]], packed_dtype=BF16)

    h_hi = h.astype(BF16)
    h_lo = (h - h_hi.astype(F32)).astype(BF16)
    logits = (jnp.dot(h_hi, rwh_ref[...], preferred_element_type=F32)
              + jnp.dot(h_lo, rwh_ref[...], preferred_element_type=F32)
              + jnp.dot(h_hi, rwl_ref[...], preferred_element_type=F32)) + rb_ref[...]

    lane = lax.broadcasted_iota(jnp.int32, logits.shape, 1).astype(F32)
    vals, idxs = [], []
    sel = jnp.zeros(logits.shape, F32)
    for _ in range(TOP_K):
        m = jnp.max(logits, axis=-1, keepdims=True)
        idx = jnp.min(jnp.where(logits == m, lane, float(V7X_LANES)), axis=-1, keepdims=True)
        hit = lane == idx
        vals.append(m)
        idxs.append(idx)
        sel = jnp.where(hit, 1.0, sel)
        logits = jnp.where(hit, -3e38, logits)
    exps = [jnp.exp(v - vals[0]) for v in vals]
    denom = exps[0] + exps[1] + exps[2] + exps[3]
    eidx = jnp.zeros(logits.shape, F32)
    gate = jnp.zeros(logits.shape, F32)
    for kk in range(TOP_K):
        eidx = jnp.where(lane == float(kk), idxs[kk], eidx)
        gate = jnp.where(lane == float(kk), exps[kk] / denom, gate)
    eidx_ref[...] = eidx.astype(jnp.int32)
    gate_ref[...] = gate
    sel_ref[...] = sel.astype(BF16)


def _out_router(a_mix, b_mix, w_out, x, mod, layer, norm_w, rw_hi, rw_lo, rb):
    n_heads, s, _ = a_mix.shape
    d = x.shape[1]
    tm = min(TM_OUT, s)
    vmem = 2 * d * d * 2 + 10 * tm * d * 4 + (8 << 20)
    row = lambda i: (i, 0)
    fixed = lambda i: (0, 0)
    lanes = jax.ShapeDtypeStruct((s, V7X_LANES), F32)
    return pl.pallas_call(
        functools.partial(_out_router_kernel, n_heads=n_heads),
        grid=(s // tm,),
        in_specs=[
            pl.BlockSpec((n_heads, tm, HEAD_DIM), lambda i: (0, i, 0)),
            pl.BlockSpec((n_heads, tm, HEAD_DIM), lambda i: (0, i, 0)),
            pl.BlockSpec((None, d, d), lambda i: (layer, 0, 0)),
            pl.BlockSpec((tm, d), row),
            pl.BlockSpec((None, 1, d), lambda i: (layer, 0, 2)),
            pl.BlockSpec((1, d), fixed),
            pl.BlockSpec((None, 1, d), lambda i: (layer, 0, 4)),
            pl.BlockSpec((None, 1, d), lambda i: (layer, 0, 3)),
            pl.BlockSpec((d, V7X_LANES), fixed),
            pl.BlockSpec((d, V7X_LANES), fixed),
            pl.BlockSpec((1, V7X_LANES), fixed),
        ],
        out_specs=[
            pl.BlockSpec((tm, d), row),
            pl.BlockSpec((tm, d // 2), row),
            pl.BlockSpec((tm, V7X_LANES), row),
            pl.BlockSpec((tm, V7X_LANES), row),
            pl.BlockSpec((tm, V7X_LANES), row),
        ],
        out_shape=[
            jax.ShapeDtypeStruct((s, d), F32),
            jax.ShapeDtypeStruct((s, d // 2), jnp.uint32),
            jax.ShapeDtypeStruct((s, V7X_LANES), jnp.int32),
            lanes,
            jax.ShapeDtypeStruct((s, V7X_LANES), BF16),
        ],
        compiler_params=_params(1, vmem),
        name="out_router",
    )(a_mix, b_mix, w_out, x, mod, norm_w, mod, mod, rw_hi, rw_lo, rb)


def _rank_kernel(sel_ref, rank_ref, count_ref, run_scr):
    i = pl.program_id(0)

    @pl.when(i == 0)
    def _():
        run_scr[...] = jnp.zeros_like(run_scr)

    sel = sel_ref[...]
    tm = sel.shape[0]
    row = lax.broadcasted_iota(jnp.int32, (tm, tm), 0)
    col = lax.broadcasted_iota(jnp.int32, (tm, tm), 1)
    before = jnp.where(col < row, 1.0, 0.0).astype(BF16)
    run = run_scr[...]
    rank_ref[...] = jnp.dot(before, sel, preferred_element_type=F32) + run
    run = run + jnp.sum(sel.astype(F32), axis=0, keepdims=True)
    run_scr[...] = run
    count_ref[...] = run


def _rank(sel):
    s = sel.shape[0]
    tm = min(TM_RANK, s)
    return pl.pallas_call(
        _rank_kernel,
        grid=(s // tm,),
        in_specs=[pl.BlockSpec((tm, V7X_LANES), lambda i: (i, 0))],
        out_specs=[pl.BlockSpec((tm, V7X_LANES), lambda i: (i, 0)),
                   pl.BlockSpec((1, V7X_LANES), lambda i: (0, 0))],
        out_shape=[jax.ShapeDtypeStruct((s, V7X_LANES), F32), jax.ShapeDtypeStruct((1, V7X_LANES), F32)],
        scratch_shapes=[pltpu.VMEM((1, V7X_LANES), F32)],
        compiler_params=_params(1, 16 * 1024 * 1024),
        name="route_rank",
    )(sel)


def _scatter_kernel(pos_ref, h_ref, xs_in_ref, xs_ref, sem, *, tm):
    del xs_in_ref

    def issue(r, _):
        for kk in range(TOP_K):
            dst = pos_ref[r * TOP_K + kk]
            pltpu.make_async_copy(h_ref.at[pl.ds(r, 1)], xs_ref.at[pl.ds(dst, 1)], sem).start()
        return ()

    lax.fori_loop(0, tm, issue, ())
    for _ in range(TOP_K):
        pltpu.make_async_copy(h_ref, xs_ref.at[pl.ds(0, tm)], sem).wait()


def _scatter_rows(pos_flat, hp, xs_zero):
    s, w = hp.shape
    tm = min(TM_COMB, s)
    return pl.pallas_call(
        functools.partial(_scatter_kernel, tm=tm),
        grid=(s // tm,),
        in_specs=[
            pl.BlockSpec((tm * TOP_K,), lambda i: (i,), memory_space=pltpu.SMEM),
            pl.BlockSpec((tm, w), lambda i: (i, 0)),
            pl.BlockSpec(memory_space=pl.ANY),
        ],
        out_specs=pl.BlockSpec(memory_space=pl.ANY),
        out_shape=jax.ShapeDtypeStruct(xs_zero.shape, xs_zero.dtype),
        scratch_shapes=[pltpu.SemaphoreType.DMA(())],
        input_output_aliases={2: 0},
        compiler_params=_params(1, 16 * 1024 * 1024),
        name="moe_scatter",
    )(pos_flat, hp, xs_zero)


def _row_tiles(n, t0, col0, x_hbm, o_hbm, xbuf, obuf, isem, osem, compute, *, tm, out_cols):
    def x_copy(t, slot):
        r0 = pl.multiple_of((t0 + t) * tm, tm)
        return pltpu.make_async_copy(x_hbm.at[pl.ds(r0, tm)], xbuf.at[slot], isem.at[slot])

    def o_copy(t, slot):
        r0 = pl.multiple_of((t0 + t) * tm, tm)
        return pltpu.make_async_copy(obuf.at[slot], o_hbm.at[pl.ds(r0, tm), pl.ds(col0, out_cols)], osem.at[slot])

    @pl.when(n > 0)
    def _():
        x_copy(0, 0).start()

    def body(t, carry):
        slot = lax.rem(t, 2)

        @pl.when(t + 1 < n)
        def _():
            x_copy(t + 1, 1 - slot).start()

        x_copy(t, slot).wait()

        @pl.when(t >= 2)
        def _():
            o_copy(t - 2, slot).wait()

        obuf[slot] = compute(xbuf[slot])
        o_copy(t, slot).start()
        return carry

    lax.fori_loop(0, n, body, 0)

    @pl.when(n >= 2)
    def _():
        o_copy(n - 2, lax.rem(n, 2)).wait()

    @pl.when(n >= 1)
    def _():
        o_copy(n - 1, lax.rem(n - 1, 2)).wait()


def _zero_fill_rows(first_tile, n_tiles, o_hbm, zbuf, zsem, *, tm):
    zbuf[...] = jnp.zeros_like(zbuf)

    def z_copy(t):
        r0 = pl.multiple_of(t * tm, tm)
        return pltpu.make_async_copy(zbuf, o_hbm.at[pl.ds(r0, tm)], zsem)

    def start(t, carry):
        z_copy(t).start()
        return carry

    def wait(t, carry):
        z_copy(t).wait()
        return carry

    lax.fori_loop(first_tile, n_tiles, start, 0)
    lax.fori_loop(first_tile, n_tiles, wait, 0)


def _moe_up_kernel(start_ref, ntile_ref, total_ref, w_ref, b_ref, x_hbm, o_hbm,
                   wb_scr, xbuf, obuf, zbuf, isem, osem, zsem, *, n_chunks, n_groups, n_tiles, tm):
    g = pl.program_id(0)
    nc = w_ref.shape[-1]

    def compute(xp):
        half = xp.shape[-1]
        x_lo = pltpu.unpack_elementwise(xp, index=0, packed_dtype=BF16, unpacked_dtype=F32).astype(BF16)
        x_hi = pltpu.unpack_elementwise(xp, index=1, packed_dtype=BF16, unpacked_dtype=F32).astype(BF16)
        gu = (jnp.dot(x_lo, wb_scr[0:half, :], preferred_element_type=F32)
              + jnp.dot(x_hi, wb_scr[half:2 * half, :], preferred_element_type=F32)) + b_ref[...]
        gate = jnp.minimum(gu, SWIGLU_LIMIT)
        glu = gate * (1.0 / (1.0 + jnp.exp(-SWIGLU_ALPHA * gate)))
        up1 = jnp.clip(gu, -SWIGLU_LIMIT, SWIGLU_LIMIT) + 1.0
        pair = 2 * V7X_LANES
        r = lax.broadcasted_iota(jnp.int32, (pair, V7X_LANES), 0)
        c = lax.broadcasted_iota(jnp.int32, (pair, V7X_LANES), 1)
        even = jnp.where(r == 2 * c, 1.0, 0.0).astype(BF16)
        outs = []
        for q in range(gu.shape[-1] // pair):
            sl = slice(q * pair, (q + 1) * pair)
            prod = glu[:, sl] * pltpu.roll(up1[:, sl], pair - 1, 1)
            outs.append(jnp.dot(prod.astype(BF16), even, preferred_element_type=F32))
        return jnp.concatenate(outs, axis=-1).astype(BF16)

    @pl.when(g < n_groups)
    def _():
        e = g // n_chunks
        col0 = pl.multiple_of(lax.rem(g, n_chunks) * (nc // 2), nc // 2)
        n = ntile_ref[e]

        @pl.when(n > 0)
        def _():
            wb_scr[...] = w_ref[...].astype(BF16)

        _row_tiles(n, start_ref[e], col0, x_hbm, o_hbm, xbuf, obuf, isem, osem, compute, tm=tm, out_cols=nc // 2)

    @pl.when(g == n_groups)
    def _():
        _zero_fill_rows(total_ref[0], n_tiles, o_hbm, zbuf, zsem, tm=tm)


def _group_specs(layer, n_chunks, n_groups, d_in, nc):
    def w_map(g, *_):
        gg = jnp.minimum(g, n_groups - 1)
        return (layer, gg // n_chunks, 0, lax.rem(gg, n_chunks))
    return [pl.BlockSpec((None, None, d_in, nc), w_map), pl.BlockSpec((None, None, 1, nc), w_map),
            pl.BlockSpec(memory_space=pl.ANY)]


def _moe_up(plan, xs, w_gate_up, b_gate_up, layer):
    p, half = xs.shape
    n_exp, d, f2 = w_gate_up.shape[1:]
    nc = min(MOE_CHUNK, f2)
    n_chunks = f2 // nc
    n_groups = n_exp * n_chunks
    tm = TM_MOE
    vmem = 2 * d * nc * 4 + d * nc * 2 + 2 * tm * half * 4 + 3 * tm * f2 + 8 * tm * nc * 4 + (6 << 20)
    grid_spec = pltpu.PrefetchScalarGridSpec(
        num_scalar_prefetch=3,
        grid=(n_groups + 1,),
        in_specs=_group_specs(layer, n_chunks, n_groups, d, nc),
        out_specs=pl.BlockSpec(memory_space=pl.ANY),
        scratch_shapes=[
            pltpu.VMEM((d, nc), BF16),
            pltpu.VMEM((2, tm, half), jnp.uint32),
            pltpu.VMEM((2, tm, nc // 2), BF16),
            pltpu.VMEM((tm, f2 // 2), BF16),
            pltpu.SemaphoreType.DMA((2,)), pltpu.SemaphoreType.DMA((2,)), pltpu.SemaphoreType.DMA(()),
        ],
    )
    return pl.pallas_call(
        functools.partial(_moe_up_kernel, n_chunks=n_chunks, n_groups=n_groups, n_tiles=p // tm, tm=tm),
        grid_spec=grid_spec,
        out_shape=jax.ShapeDtypeStruct((p, f2 // 2), BF16),
        compiler_params=_params(1, vmem),
        name="moe_up",
    )(*plan, w_gate_up, b_gate_up.reshape(b_gate_up.shape[0], n_exp, 1, f2), xs)


def _moe_down_kernel(start_ref, ntile_ref, total_ref, w_ref, b_ref, x_hbm, o_hbm,
                     wb_scr, xbuf, obuf, zbuf, isem, osem, zsem, *, n_chunks, n_groups, n_tiles, tm):
    g = pl.program_id(0)
    nc = w_ref.shape[-1]

    def compute(a):
        return jnp.dot(a, wb_scr[...], preferred_element_type=F32) + b_ref[...]

    @pl.when(g < n_groups)
    def _():
        e = g // n_chunks
        col0 = pl.multiple_of(lax.rem(g, n_chunks) * nc, nc)
        n = ntile_ref[e]

        @pl.when(n > 0)
        def _():
            wb_scr[...] = w_ref[...].astype(BF16)

        _row_tiles(n, start_ref[e], col0, x_hbm, o_hbm, xbuf, obuf, isem, osem, compute, tm=tm, out_cols=nc)

    @pl.when(g == n_groups)
    def _():
        _zero_fill_rows(total_ref[0], n_tiles, o_hbm, zbuf, zsem, tm=tm)


def _moe_down(plan, act, w_down, b_down, layer):
    p, f = act.shape
    n_exp, _, d = w_down.shape[1:]
    nc = min(MOE_CHUNK, d)
    n_chunks = d // nc
    n_groups = n_exp * n_chunks
    tm = TM_MOE
    vmem = 2 * f * nc * 4 + f * nc * 2 + 2 * tm * f * 2 + tm * d * 4 + 5 * tm * nc * 4 + (6 << 20)
    grid_spec = pltpu.PrefetchScalarGridSpec(
        num_scalar_prefetch=3,
        grid=(n_groups + 1,),
        in_specs=_group_specs(layer, n_chunks, n_groups, f, nc),
        out_specs=pl.BlockSpec(memory_space=pl.ANY),
        scratch_shapes=[
            pltpu.VMEM((f, nc), BF16),
            pltpu.VMEM((2, tm, f), BF16),
            pltpu.VMEM((2, tm, nc), F32),
            pltpu.VMEM((tm, d), F32),
            pltpu.SemaphoreType.DMA((2,)), pltpu.SemaphoreType.DMA((2,)), pltpu.SemaphoreType.DMA(()),
        ],
    )
    return pl.pallas_call(
        functools.partial(_moe_down_kernel, n_chunks=n_chunks, n_groups=n_groups, n_tiles=p // tm, tm=tm),
        grid_spec=grid_spec,
        out_shape=jax.ShapeDtypeStruct((p, d), F32),
        compiler_params=_params(1, vmem),
        name="moe_down",
    )(*plan, w_down, b_down.reshape(b_down.shape[0], n_exp, 1, d), act)


def _moe_plan(counts):
    tiles_per = (counts + TM_MOE - 1) // TM_MOE
    tile_end = jnp.cumsum(tiles_per)
    tile_start = tile_end - tiles_per
    i32 = lambda a: a.astype(jnp.int32)
    return i32(tile_start), i32(tiles_per), i32(tile_end[-1:])


def _combine_kernel(pos_ref, ys_ref, x_ref, gate_ref, gf_ref, nw_ref, o_ref, buf, sem, *, tm, final):
    def issue(r, _):
        for kk in range(TOP_K):
            src = pos_ref[r * TOP_K + kk]
            pltpu.make_async_copy(ys_ref.at[pl.ds(src, 1)], buf.at[kk, pl.ds(r, 1)], sem).start()
        return ()

    lax.fori_loop(0, tm, issue, ())
    for kk in range(TOP_K):
        pltpu.make_async_copy(ys_ref.at[pl.ds(0, tm)], buf.at[kk], sem).wait()

    gates = gate_ref[...]
    y = gates[:, 0:1] * buf[0]
    for kk in range(1, TOP_K):
        y = y + gates[:, kk:kk + 1] * buf[kk]
    out = x_ref[...] + gf_ref[...] * y
    if final:
        out = _rms(out) * nw_ref[...]
    o_ref[...] = out


def _combine(pos_flat, ys, x1, gates, mod, layer, norm_final, final):
    s, d = x1.shape
    tm = min(TM_COMB, s)
    vmem = TOP_K * tm * d * 4 + 6 * tm * d * 4 + (6 << 20)
    return pl.pallas_call(
        functools.partial(_combine_kernel, tm=tm, final=final),
        grid=(s // tm,),
        in_specs=[
            pl.BlockSpec((tm * TOP_K,), lambda i: (i,), memory_space=pltpu.SMEM),
            pl.BlockSpec(memory_space=pl.ANY),
            pl.BlockSpec((tm, d), lambda i: (i, 0)),
            pl.BlockSpec((tm, V7X_LANES), lambda i: (i, 0)),
            pl.BlockSpec((None, 1, d), lambda i: (layer, 0, 5)),
            pl.BlockSpec((1, d), lambda i: (0, 0)),
        ],
        out_specs=pl.BlockSpec((tm, d), lambda i: (i, 0)),
        out_shape=jax.ShapeDtypeStruct((s, d), F32),
        scratch_shapes=[pltpu.VMEM((TOP_K, tm, d), F32), pltpu.SemaphoreType.DMA(())],
        compiler_params=_params(1, vmem),
        name="moe_combine",
    )(pos_flat, ys, x1, gates, mod, norm_final)


def kernel(x, c, positions, ada_w, ada_b, norm_mix, w_in, w_out, lambda_q1, lambda_k1, lambda_q2, lambda_k2,
           subln_w, sb_norm_w, norm_ffn, router_w, router_b, w_gate_up, b_gate_up, w_down, b_down, norm_final):
    batch, s, d = x.shape
    assert batch == 1, "the kernels are written for one sequence"
    depth = ada_w.shape[0]
    n_exp = router_w.shape[-1]
    assert n_exp <= V7X_LANES

    xc = x.reshape(s, d)
    mod = _ada_mod(c.reshape(d, 1), ada_w, ada_b)
    cos, sin = _rope_tables(positions.reshape(s, 1).astype(F32))
    w_in_b = w_in.astype(BF16)
    w_out_b = w_out.astype(BF16)

    n_tiles = (s * TOP_K) // TM_MOE + n_exp
    n_rows = n_tiles * TM_MOE
    pad = V7X_LANES - n_exp
    row = lambda a: a.reshape(1, -1)

    for l in range(depth):
        lam_init = 0.8 - 0.6 * math.exp(-0.3 * l)
        lam = (jnp.exp(jnp.sum(lambda_q1[l] * lambda_k1[l])) - jnp.exp(jnp.sum(lambda_q2[l] * lambda_k2[l]))
               + lam_init).reshape(1).astype(F32)

        qkv = _in_proj(xc, mod, l, row(norm_mix[l]), w_in_b, cos, sin)
        a_mix = _diff_attn(qkv, lam, row(subln_w[l]), 1.0 - lam_init)
        b_mix = _sb_attn(qkv, row(sb_norm_w[l]))

        rw = jnp.pad(router_w[l], ((0, 0), (0, pad)))
        rw_hi = rw.astype(BF16)
        rw_lo = (rw - rw_hi.astype(F32)).astype(BF16)
        rb = jnp.pad(router_b[l], (0, pad), constant_values=NEG_BIG).reshape(1, V7X_LANES)
        x1, hp, eidx, gates, sel = _out_router(a_mix, b_mix, w_out_b, xc, mod, l, row(norm_ffn[l]),
                                               rw_hi, rw_lo, rb)

        rank, counts = _rank(sel)
        counts = counts[0, :n_exp].astype(jnp.int32)
        plan = _moe_plan(counts)
        slot = rank[:, :n_exp] + (plan[0] * TM_MOE).astype(F32)[None, :]
        pos = jnp.take_along_axis(slot, eidx[:, :TOP_K], axis=1).astype(jnp.int32).reshape(-1)

        xs = _scatter_rows(pos, hp, jnp.zeros((n_rows, d // 2), jnp.uint32))
        act = _moe_up(plan, xs, w_gate_up, b_gate_up, l)
        ys = _moe_down(plan, act, w_down, b_down, l)
        xc = _combine(pos, ys, x1, gates, mod, l, row(norm_final), l == depth - 1)

    return xc.reshape(batch, s, d)
```

```python
import functools
import math

import jax
import jax.numpy as jnp
from jax import lax
from jax.experimental import pallas as pl
from jax.experimental.pallas import tpu as pltpu

F32 = jnp.float32
BF16 = jnp.bfloat16

HEAD_DIM = 128
DIFF_QK_DIM = HEAD_DIM // 2
N_MOD = 6
TOP_K = 4
SWIGLU_LIMIT = 7.0
SWIGLU_ALPHA = 1.702
ROPE_THETA = 10000.0
RMS_EPS = 1e-5

V7X_LANES = 128
V7X_VMEM_BYTES = 64 * 1024 * 1024

TM_PROJ = 512
TQ_ATTN = 256
TQ_DIFF = 512
SB_HEADS_PER_STEP = 4
DIFF_HEADS_PER_STEP = 2
TM_OUT = 256
TM_RANK = 256
TM_MOE = 256
TM_COMB = 256
MOE_CHUNK = 1024
MOD_CHUNK = 1024
W_SLABS = 4
NEG_BIG = -1e30
SB_EXIT_LOG = -100.0

_NT_DIMS = (((1,), (1,)), ((), ()))


def _params(n_axes, vmem_bytes):
    return pltpu.CompilerParams(
        dimension_semantics=("arbitrary",) * n_axes,
        vmem_limit_bytes=min(int(vmem_bytes), V7X_VMEM_BYTES - 8 * 1024 * 1024),
    )


def _rms(x):
    return x * lax.rsqrt(jnp.mean(x * x, axis=-1, keepdims=True) + RMS_EPS)


def _mod_kernel(c_ref, w_ref, b_ref, o_ref):
    c = c_ref[...]
    ca = c * (1.0 / (1.0 + jnp.exp(-c)))
    o_ref[...] = jnp.sum(ca * w_ref[...], axis=0, keepdims=True) + b_ref[...]


def _ada_mod(c_col, ada_w, ada_b):
    depth, d, n = ada_w.shape
    nc = min(MOD_CHUNK, n)
    return pl.pallas_call(
        _mod_kernel,
        grid=(depth, n // nc),
        in_specs=[
            pl.BlockSpec((d, 1), lambda l, j: (0, 0)),
            pl.BlockSpec((None, d, nc), lambda l, j: (l, 0, j)),
            pl.BlockSpec((None, 1, nc), lambda l, j: (l, 0, j)),
        ],
        out_specs=pl.BlockSpec((None, 1, nc), lambda l, j: (l, 0, j)),
        out_shape=jax.ShapeDtypeStruct((depth, 1, n), F32),
        compiler_params=_params(2, 3 * d * nc * 4 + 4 * d * V7X_LANES * 4),
        name="ada_mod",
    )(c_col, ada_w, ada_b.reshape(depth, 1, n))


def _rope_kernel(pos_ref, invf_ref, sign_ref, cos_ref, sin_ref):
    ang = pos_ref[...] * invf_ref[...]
    cos_ref[...] = jnp.cos(ang)
    sin_ref[...] = jnp.sin(ang) * sign_ref[...]


def _rope_tables(pos_col):
    s = pos_col.shape[0]
    half = DIFF_QK_DIM // 2
    inv_freq = ROPE_THETA ** (-jnp.arange(half, dtype=F32) / half)
    reps = V7X_LANES // half
    invf = jnp.tile(inv_freq, reps).reshape(1, V7X_LANES)
    sign = jnp.tile(jnp.concatenate([-jnp.ones((half,), F32), jnp.ones((half,), F32)]), reps // 2)
    ts = min(1024, s)
    tab = jax.ShapeDtypeStruct((s, V7X_LANES), F32)
    return pl.pallas_call(
        _rope_kernel,
        grid=(s // ts,),
        in_specs=[
            pl.BlockSpec((ts, 1), lambda i: (i, 0)),
            pl.BlockSpec((1, V7X_LANES), lambda i: (0, 0)),
            pl.BlockSpec((1, V7X_LANES), lambda i: (0, 0)),
        ],
        out_specs=[pl.BlockSpec((ts, V7X_LANES), lambda i: (i, 0))] * 2,
        out_shape=[tab, tab],
        compiler_params=_params(1, 32 * 1024 * 1024),
        name="rope_tables",
    )(pos_col, invf, sign.reshape(1, V7X_LANES))


def _in_proj_kernel(x_ref, nw_ref, sc_ref, sh_ref, w_ref, cos_ref, sin_ref, o_ref, h_scr, *, n_heads):
    j = pl.program_id(1)

    @pl.when(j == 0)
    def _():
        h = _rms(x_ref[...]) * nw_ref[...]
        h_scr[...] = (h * (1.0 + sc_ref[...]) + sh_ref[...]).astype(BF16)

    res = jnp.dot(h_scr[...], w_ref[...], preferred_element_type=F32)

    def store(scale, rotary):
        if rotary:
            cos = cos_ref[...]
            sin = sin_ref[...]
            lane = lax.broadcasted_iota(jnp.int32, cos.shape, 1)
            first_half = (lane % DIFF_QK_DIM) < (DIFF_QK_DIM // 2)
        for g in range(n_heads):
            blk = res[:, g * HEAD_DIM:(g + 1) * HEAD_DIM]
            if rotary:
                fwd = pltpu.roll(blk, HEAD_DIM - DIFF_QK_DIM // 2, 1)
                bwd = pltpu.roll(blk, DIFF_QK_DIM // 2, 1)
                blk = blk * cos + jnp.where(first_half, fwd, bwd) * sin
            if scale != 1.0:
                blk = blk * scale
            o_ref[g] = blk.astype(BF16)

    pl.when(j == 0)(lambda: store(DIFF_QK_DIM ** -0.5 * math.log2(math.e), True))
    pl.when(j == 1)(lambda: store(1.0, True))
    pl.when(j == 3)(lambda: store(HEAD_DIM ** -0.5, False))
    pl.when((j == 2) | (j >= 4))(lambda: store(1.0, False))


def _in_proj(x, mod, layer, norm_w, w_in, cos, sin):
    s, d = x.shape
    gw = w_in.shape[-1] // 6
    n_heads = gw // HEAD_DIM
    tm = min(TM_PROJ, s)
    vmem = 2 * tm * d * 4 + tm * d * 2 + 2 * d * gw * 2 + 2 * tm * gw * 2 + 3 * tm * gw * 4 + (4 << 20)
    return pl.pallas_call(
        functools.partial(_in_proj_kernel, n_heads=n_heads),
        grid=(s // tm, 6),
        in_specs=[
            pl.BlockSpec((tm, d), lambda i, j: (i, 0)),
            pl.BlockSpec((1, d), lambda i, j: (0, 0)),
            pl.BlockSpec((None, 1, d), lambda i, j: (layer, 0, 1)),
            pl.BlockSpec((None, 1, d), lambda i, j: (layer, 0, 0)),
            pl.BlockSpec((None, d, gw), lambda i, j: (layer, 0, j)),
            pl.BlockSpec((tm, V7X_LANES), lambda i, j: (i, 0)),
            pl.BlockSpec((tm, V7X_LANES), lambda i, j: (i, 0)),
        ],
        out_specs=pl.BlockSpec((n_heads, tm, HEAD_DIM), lambda i, j: (j, i, 0)),
        out_shape=jax.ShapeDtypeStruct((6 * n_heads, s, HEAD_DIM), BF16),
        scratch_shapes=[pltpu.VMEM((tm, d), BF16)],
        compiler_params=_params(2, vmem),
        name="in_proj",
    )(x, norm_w, mod, mod, w_in, cos, sin)


def _diff_attn_kernel(lam_ref, q_ref, k_ref, v_ref, w_ref, o_ref, m_scr, l_scr, acc_scr, *, tq, out_scale, n_group):
    qi = pl.program_id(1)
    lane = lax.broadcasted_iota(jnp.int32, (tq, HEAD_DIM), 1)
    n_chunks = tq // V7X_LANES
    m_scr[...] = jnp.full(m_scr.shape, NEG_BIG, F32)
    l_scr[...] = jnp.zeros(l_scr.shape, F32)
    acc_scr[...] = jnp.zeros(acc_scr.shape, F32)

    def step(ki, diag):
        start = pl.multiple_of(ki * tq, tq)
        for h in range(n_group):
            q = q_ref[h]
            zero = jnp.zeros_like(q)
            k = k_ref[h, pl.ds(start, tq), :]
            v = v_ref[h, pl.ds(start, tq), :]
            for p in range(2):
                q_sub = jnp.where((lane < DIFF_QK_DIM) == (p == 0), q, zero)
                s = lax.dot_general(q_sub, k, _NT_DIMS, preferred_element_type=F32)
                if diag:
                    row = lax.broadcasted_iota(jnp.int32, s.shape, 0)
                    col = lax.broadcasted_iota(jnp.int32, s.shape, 1)
                    s = jnp.where(col <= row, s, NEG_BIG)
                chunks = [s[:, c * V7X_LANES:(c + 1) * V7X_LANES] for c in range(n_chunks)]
                i = 2 * h + p
                m = m_scr[i]
                m_new = jnp.maximum(m, jnp.max(functools.reduce(jnp.maximum, chunks), axis=-1, keepdims=True))
                alpha = jnp.exp2(m - m_new)
                pe = [jnp.exp2(ch - m_new) for ch in chunks]
                l_scr[i] = alpha * l_scr[i] + functools.reduce(jnp.add, pe)
                pb = jnp.concatenate([x.astype(BF16) for x in pe], axis=-1)
                acc_scr[i] = alpha * acc_scr[i] + jnp.dot(pb, v, preferred_element_type=F32)
                m_scr[i] = m_new

    def body(ki, c):
        step(ki, False)
        return c

    lax.fori_loop(0, qi, body, 0)
    step(qi, True)
    for h in range(n_group):
        den = [jnp.sum(l_scr[2 * h + p], axis=-1, keepdims=True) for p in range(2)]
        a = acc_scr[2 * h] / den[0] - lam_ref[0] * (acc_scr[2 * h + 1] / den[1])
        o_ref[h] = (_rms(a) * (w_ref[...] * out_scale)).astype(BF16)


def _diff_attn(qkv, lam, subln_w, out_scale):
    n_heads = qkv.shape[0] // 6
    s = qkv.shape[1]
    tq = min(TQ_DIFF, s)
    n_group = math.gcd(DIFF_HEADS_PER_STEP, n_heads)
    vmem = 4 * n_group * s * HEAD_DIM * 2 + n_group * (16 * tq * tq * 4 + 6 * tq * V7X_LANES * 4) + (8 << 20)
    kv = lambda base: (lambda hg, i: (base * n_heads // n_group + hg, 0, 0))
    return pl.pallas_call(
        functools.partial(_diff_attn_kernel, tq=tq, out_scale=out_scale, n_group=n_group),
        grid=(n_heads // n_group, s // tq),
        in_specs=[
            pl.BlockSpec(memory_space=pltpu.SMEM),
            pl.BlockSpec((n_group, tq, HEAD_DIM), lambda hg, i: (hg, i, 0)),
            pl.BlockSpec((n_group, s, HEAD_DIM), kv(1)),
            pl.BlockSpec((n_group, s, HEAD_DIM), kv(2)),
            pl.BlockSpec((1, HEAD_DIM), lambda hg, i: (0, 0)),
        ],
        out_specs=pl.BlockSpec((n_group, tq, HEAD_DIM), lambda hg, i: (hg, i, 0)),
        out_shape=jax.ShapeDtypeStruct((n_heads, s, HEAD_DIM), BF16),
        scratch_shapes=[pltpu.VMEM((2 * n_group, tq, V7X_LANES), F32), pltpu.VMEM((2 * n_group, tq, V7X_LANES), F32),
                        pltpu.VMEM((2 * n_group, tq, HEAD_DIM), F32)],
        compiler_params=_params(2, vmem),
        name="diff_attn",
    )(lam, qkv, qkv, qkv, subln_w)


def _sb_attn_kernel(q_ref, k_ref, v_ref, w_ref, o_ref, *, tq, n_group):
    qi = pl.program_id(1)
    n_chunks = tq // V7X_LANES
    row = lax.broadcasted_iota(jnp.int32, (tq, tq), 0)
    col = lax.broadcasted_iota(jnp.int32, (tq, tq), 1)
    strict = col < row
    after = jnp.where(row > col, 1.0, 0.0).astype(BF16)
    ones = jnp.ones((tq, V7X_LANES), BF16)

    def block(h, ki, rem, acc, diag):
        start = pl.multiple_of(ki * tq, tq)
        k = k_ref[h, pl.ds(start, tq), :]
        v = v_ref[h, pl.ds(start, tq), :]
        z = lax.dot_general(q_ref[h], k, _NT_DIMS, preferred_element_type=F32)
        t = jnp.log1p(jnp.exp(-jnp.abs(z)))
        log_beta = jnp.minimum(z, 0.0) - t
        log_1m = -jnp.maximum(z, 0.0) - t
        if diag:
            log_1m = jnp.where(strict, log_1m, 0.0)
        hi = log_1m.astype(BF16)
        lo = (log_1m - hi.astype(F32)).astype(BF16)
        later = (jnp.dot(hi, after, preferred_element_type=F32)
                 + jnp.dot(lo, after, preferred_element_type=F32))
        total = (jnp.dot(hi, ones, preferred_element_type=F32)
                 + jnp.dot(lo, ones, preferred_element_type=F32))
        log_a = log_beta + later
        a = []
        for c in range(n_chunks):
            sl = slice(c * V7X_LANES, (c + 1) * V7X_LANES)
            a_c = jnp.exp(log_a[:, sl] + rem)
            if diag:
                a_c = jnp.where(strict[:, sl], a_c, 0.0)
            a.append(a_c.astype(BF16))
        acc = acc + jnp.dot(jnp.concatenate(a, axis=-1), v, preferred_element_type=F32)
        return rem + total, acc

    zero = jnp.zeros((tq, V7X_LANES), F32)
    state = [block(h, qi, zero, zero, True) for h in range(n_group)]
    rems = tuple(s[0] for s in state)
    accs = tuple(s[1] for s in state)

    def cond(c):
        ki, rems, _ = c
        return jnp.logical_and(ki >= 0, jnp.max(functools.reduce(jnp.maximum, rems)) > SB_EXIT_LOG)

    def body(c):
        ki, rems, accs = c
        state = [block(h, ki, rems[h], accs[h], False) for h in range(n_group)]
        return ki - 1, tuple(s[0] for s in state), tuple(s[1] for s in state)

    _, _, accs = lax.while_loop(cond, body, (qi - 1, rems, accs))
    for h in range(n_group):
        o_ref[h] = (_rms(accs[h]) * w_ref[...]).astype(BF16)


def _sb_attn(qkv, norm_w):
    n_heads = qkv.shape[0] // 6
    s = qkv.shape[1]
    tq = min(TQ_ATTN, s)
    n_group = math.gcd(SB_HEADS_PER_STEP, n_heads)
    vmem = 4 * n_group * s * HEAD_DIM * 2 + n_group * 12 * tq * tq * 4 + (8 << 20)
    blk = lambda base: (lambda hg, i: (base * n_heads // n_group + hg, 0, 0))
    return pl.pallas_call(
        functools.partial(_sb_attn_kernel, tq=tq, n_group=n_group),
        grid=(n_heads // n_group, s // tq),
        in_specs=[
            pl.BlockSpec((n_group, tq, HEAD_DIM), lambda hg, i: (3 * n_heads // n_group + hg, i, 0)),
            pl.BlockSpec((n_group, s, HEAD_DIM), blk(4)),
            pl.BlockSpec((n_group, s, HEAD_DIM), blk(5)),
            pl.BlockSpec((1, HEAD_DIM), lambda hg, i: (0, 0)),
        ],
        out_specs=pl.BlockSpec((n_group, tq, HEAD_DIM), lambda hg, i: (hg, i, 0)),
        out_shape=jax.ShapeDtypeStruct((n_heads, s, HEAD_DIM), BF16),
        compiler_params=_params(2, vmem),
        name="sb_attn",
    )(qkv, qkv, qkv, norm_w)


def _out_router_kernel(a_ref, b_ref, w_ref, x_ref, ga_ref, nw_ref, sc_ref, sh_ref, rwh_ref, rwl_ref, rb_ref,
                       x1_ref, hp_ref, eidx_ref, gate_ref, sel_ref, *, n_heads):
    mix = jnp.concatenate([a_ref[g] for g in range(n_heads)] + [b_ref[g] for g in range(n_heads)], axis=-1)
    y = jnp.dot(mix, w_ref[...], preferred_element_type=F32)
    x1 = x_ref[...] + ga_ref[...] * y
    x1_ref[...] = x1
    h = _rms(x1) * nw_ref[...]
    h = h * (1.0 + sc_ref[...]) + sh_ref[...]
    half = h.shape[-1] // 2
    hp_ref[...] = pltpu.pack_elementwise([h[:, :half], h[:, half:]], packed_dtype=BF16)

    h_hi = h.astype(BF16)
    h_lo = (h - h_hi.astype(F32)).astype(BF16)
    logits = (jnp.dot(h_hi, rwh_ref[...], preferred_element_type=F32)
              + jnp.dot(h_lo, rwh_ref[...], preferred_element_type=F32)
              + jnp.dot(h_hi, rwl_ref[...], preferred_element_type=F32)) + rb_ref[...]

    lane = lax.broadcasted_iota(jnp.int32, logits.shape, 1).astype(F32)
    vals, idxs = [], []
    sel = jnp.zeros(logits.shape, F32)
    for _ in range(TOP_K):
        m = jnp.max(logits, axis=-1, keepdims=True)
        idx = jnp.min(jnp.where(logits == m, lane, float(V7X_LANES)), axis=-1, keepdims=True)
        hit = lane == idx
        vals.append(m)
        idxs.append(idx)
        sel = jnp.where(hit, 1.0, sel)
        logits = jnp.where(hit, -3e38, logits)
    exps = [jnp.exp(v - vals[0]) for v in vals]
    denom = exps[0] + exps[1] + exps[2] + exps[3]
    eidx = jnp.zeros(logits.shape, F32)
    gate = jnp.zeros(logits.shape, F32)
    for kk in range(TOP_K):
        eidx = jnp.where(lane == float(kk), idxs[kk], eidx)
        gate = jnp.where(lane == float(kk), exps[kk] / denom, gate)
    eidx_ref[...] = eidx.astype(jnp.int32)
    gate_ref[...] = gate
    sel_ref[...] = sel.astype(BF16)


def _out_router(a_mix, b_mix, w_out, x, mod, layer, norm_w, rw_hi, rw_lo, rb):
    n_heads, s, _ = a_mix.shape
    d = x.shape[1]
    tm = min(TM_OUT, s)
    vmem = 2 * d * d * 2 + 10 * tm * d * 4 + (8 << 20)
    row = lambda i: (i, 0)
    fixed = lambda i: (0, 0)
    lanes = jax.ShapeDtypeStruct((s, V7X_LANES), F32)
    return pl.pallas_call(
        functools.partial(_out_router_kernel, n_heads=n_heads),
        grid=(s // tm,),
        in_specs=[
            pl.BlockSpec((n_heads, tm, HEAD_DIM), lambda i: (0, i, 0)),
            pl.BlockSpec((n_heads, tm, HEAD_DIM), lambda i: (0, i, 0)),
            pl.BlockSpec((None, d, d), lambda i: (layer, 0, 0)),
            pl.BlockSpec((tm, d), row),
            pl.BlockSpec((None, 1, d), lambda i: (layer, 0, 2)),
            pl.BlockSpec((1, d), fixed),
            pl.BlockSpec((None, 1, d), lambda i: (layer, 0, 4)),
            pl.BlockSpec((None, 1, d), lambda i: (layer, 0, 3)),
            pl.BlockSpec((d, V7X_LANES), fixed),
            pl.BlockSpec((d, V7X_LANES), fixed),
            pl.BlockSpec((1, V7X_LANES), fixed),
        ],
        out_specs=[
            pl.BlockSpec((tm, d), row),
            pl.BlockSpec((tm, d // 2), row),
            pl.BlockSpec((tm, V7X_LANES), row),
            pl.BlockSpec((tm, V7X_LANES), row),
            pl.BlockSpec((tm, V7X_LANES), row),
        ],
        out_shape=[
            jax.ShapeDtypeStruct((s, d), F32),
            jax.ShapeDtypeStruct((s, d // 2), jnp.uint32),
            jax.ShapeDtypeStruct((s, V7X_LANES), jnp.int32),
            lanes,
            jax.ShapeDtypeStruct((s, V7X_LANES), BF16),
        ],
        compiler_params=_params(1, vmem),
        name="out_router",
    )(a_mix, b_mix, w_out, x, mod, norm_w, mod, mod, rw_hi, rw_lo, rb)


def _rank_kernel(sel_ref, rank_ref, count_ref, run_scr):
    i = pl.program_id(0)

    @pl.when(i == 0)
    def _():
        run_scr[...] = jnp.zeros_like(run_scr)

    sel = sel_ref[...]
    tm = sel.shape[0]
    row = lax.broadcasted_iota(jnp.int32, (tm, tm), 0)
    col = lax.broadcasted_iota(jnp.int32, (tm, tm), 1)
    before = jnp.where(col < row, 1.0, 0.0).astype(BF16)
    run = run_scr[...]
    rank_ref[...] = jnp.dot(before, sel, preferred_element_type=F32) + run
    run = run + jnp.sum(sel.astype(F32), axis=0, keepdims=True)
    run_scr[...] = run
    count_ref[...] = run


def _rank(sel):
    s = sel.shape[0]
    tm = min(TM_RANK, s)
    return pl.pallas_call(
        _rank_kernel,
        grid=(s // tm,),
        in_specs=[pl.BlockSpec((tm, V7X_LANES), lambda i: (i, 0))],
        out_specs=[pl.BlockSpec((tm, V7X_LANES), lambda i: (i, 0)),
                   pl.BlockSpec((1, V7X_LANES), lambda i: (0, 0))],
        out_shape=[jax.ShapeDtypeStruct((s, V7X_LANES), F32), jax.ShapeDtypeStruct((1, V7X_LANES), F32)],
        scratch_shapes=[pltpu.VMEM((1, V7X_LANES), F32)],
        compiler_params=_params(1, 16 * 1024 * 1024),
        name="route_rank",
    )(sel)


def _scatter_kernel(pos_ref, h_ref, xs_in_ref, xs_ref, sem, *, tm):
    del xs_in_ref

    def issue(r, _):
        for kk in range(TOP_K):
            dst = pos_ref[r * TOP_K + kk]
            pltpu.make_async_copy(h_ref.at[pl.ds(r, 1)], xs_ref.at[pl.ds(dst, 1)], sem).start()
        return ()

    lax.fori_loop(0, tm, issue, ())
    for _ in range(TOP_K):
        pltpu.make_async_copy(h_ref, xs_ref.at[pl.ds(0, tm)], sem).wait()


def _scatter_rows(pos_flat, hp, xs_zero):
    s, w = hp.shape
    tm = min(TM_COMB, s)
    return pl.pallas_call(
        functools.partial(_scatter_kernel, tm=tm),
        grid=(s // tm,),
        in_specs=[
            pl.BlockSpec((tm * TOP_K,), lambda i: (i,), memory_space=pltpu.SMEM),
            pl.BlockSpec((tm, w), lambda i: (i, 0)),
            pl.BlockSpec(memory_space=pl.ANY),
        ],
        out_specs=pl.BlockSpec(memory_space=pl.ANY),
        out_shape=jax.ShapeDtypeStruct(xs_zero.shape, xs_zero.dtype),
        scratch_shapes=[pltpu.SemaphoreType.DMA(())],
        input_output_aliases={2: 0},
        compiler_params=_params(1, 16 * 1024 * 1024),
        name="moe_scatter",
    )(pos_flat, hp, xs_zero)


def _row_tiles(n, t0, col0, x_hbm, o_hbm, xbuf, obuf, isem, osem, compute, before_loop, each_tile, *, tm, out_cols):
    def x_copy(t, slot):
        r0 = pl.multiple_of((t0 + t) * tm, tm)
        return pltpu.make_async_copy(x_hbm.at[pl.ds(r0, tm)], xbuf.at[slot], isem.at[slot])

    def o_copy(t, slot):
        r0 = pl.multiple_of((t0 + t) * tm, tm)
        return pltpu.make_async_copy(obuf.at[slot], o_hbm.at[pl.ds(r0, tm), pl.ds(col0, out_cols)], osem.at[slot])

    @pl.when(n > 0)
    def _():
        x_copy(0, 0).start()

    before_loop()

    def body(t, carry):
        slot = lax.rem(t, 2)

        @pl.when(t + 1 < n)
        def _():
            x_copy(t + 1, 1 - slot).start()

        each_tile(t)
        x_copy(t, slot).wait()

        @pl.when(t >= 2)
        def _():
            o_copy(t - 2, slot).wait()

        obuf[slot] = compute(xbuf[slot])
        o_copy(t, slot).start()
        return carry

    lax.fori_loop(0, n, body, 0)

    @pl.when(n >= 2)
    def _():
        o_copy(n - 2, lax.rem(n, 2)).wait()

    @pl.when(n >= 1)
    def _():
        o_copy(n - 1, lax.rem(n - 1, 2)).wait()


def _zero_fill_rows(first_tile, n_tiles, o_hbm, zbuf, zsem, *, tm):
    zbuf[...] = jnp.zeros_like(zbuf)

    def z_copy(t):
        r0 = pl.multiple_of(t * tm, tm)
        return pltpu.make_async_copy(zbuf, o_hbm.at[pl.ds(r0, tm)], zsem)

    def start(t, carry):
        z_copy(t).start()
        return carry

    def wait(t, carry):
        z_copy(t).wait()
        return carry

    lax.fori_loop(first_tile, n_tiles, start, 0)
    lax.fori_loop(first_tile, n_tiles, wait, 0)


def _expert_groups(layer, start_ref, ntile_ref, total_ref, w_hbm, x_hbm, o_hbm, wbuf, wb_scr, xbuf, obuf, zbuf,
                   wsem, isem, osem, zsem, compute, *, n_chunks, n_groups, n_tiles, tm, out_cols):
    g = pl.program_id(0)
    d_in, nc = wb_scr.shape
    rows = d_in // W_SLABS

    def tiles_of(gg):
        return ntile_ref[gg // n_chunks]

    def w_copy(gg, k):
        col = pl.multiple_of(lax.rem(gg, n_chunks) * nc, nc)
        r0 = pl.multiple_of(k * rows, rows)
        slot = lax.rem(gg, 2)
        return pltpu.make_async_copy(w_hbm.at[layer, gg // n_chunks, pl.ds(r0, rows), pl.ds(col, nc)],
                                     wbuf.at[slot, pl.ds(r0, rows)], wsem.at[slot])

    @pl.when(jnp.logical_and(g == 0, tiles_of(0) > 0))
    def _():
        for k in range(W_SLABS):
            w_copy(0, k).start()

    @pl.when(g < n_groups)
    def _():
        n = tiles_of(g)
        nxt = jnp.minimum(g + 1, n_groups - 1)
        fetch_next = jnp.logical_and(g + 1 < n_groups, tiles_of(nxt) > 0)
        col0 = pl.multiple_of(lax.rem(g, n_chunks) * out_cols, out_cols)

        def before_loop():
            @pl.when(n > 0)
            def _():
                for k in range(W_SLABS):
                    w_copy(g, k).wait()
                wb_scr[...] = wbuf[lax.rem(g, 2)].astype(BF16)

        def each_tile(t):
            @pl.when(jnp.logical_and(fetch_next, t < W_SLABS))
            def _():
                w_copy(nxt, t).start()

        _row_tiles(n, start_ref[g // n_chunks], col0, x_hbm, o_hbm, xbuf, obuf, isem, osem, compute,
                   before_loop, each_tile, tm=tm, out_cols=out_cols)

        for k in range(W_SLABS):
            @pl.when(jnp.logical_and(fetch_next, n <= k))
            def _():
                w_copy(nxt, k).start()

    @pl.when(g == n_groups)
    def _():
        _zero_fill_rows(total_ref[0], n_tiles, o_hbm, zbuf, zsem, tm=tm)


def _moe_up_kernel(start_ref, ntile_ref, total_ref, b_ref, w_hbm, x_hbm, o_hbm,
                   wbuf, wb_scr, xbuf, obuf, zbuf, wsem, isem, osem, zsem, *, layer, n_chunks, n_groups, n_tiles, tm):
    nc = wb_scr.shape[-1]

    def compute(xp):
        half = xp.shape[-1]
        x_lo = pltpu.unpack_elementwise(xp, index=0, packed_dtype=BF16, unpacked_dtype=F32).astype(BF16)
        x_hi = pltpu.unpack_elementwise(xp, index=1, packed_dtype=BF16, unpacked_dtype=F32).astype(BF16)
        gu = (jnp.dot(x_lo, wb_scr[0:half, :], preferred_element_type=F32)
              + jnp.dot(x_hi, wb_scr[half:2 * half, :], preferred_element_type=F32)) + b_ref[...]
        gate = jnp.minimum(gu, SWIGLU_LIMIT)
        glu = gate * (1.0 / (1.0 + jnp.exp(-SWIGLU_ALPHA * gate)))
        up1 = jnp.clip(gu, -SWIGLU_LIMIT, SWIGLU_LIMIT) + 1.0
        pair = 2 * V7X_LANES
        r = lax.broadcasted_iota(jnp.int32, (pair, V7X_LANES), 0)
        c = lax.broadcasted_iota(jnp.int32, (pair, V7X_LANES), 1)
        even = jnp.where(r == 2 * c, 1.0, 0.0).astype(BF16)
        outs = []
        for q in range(gu.shape[-1] // pair):
            sl = slice(q * pair, (q + 1) * pair)
            prod = glu[:, sl] * pltpu.roll(up1[:, sl], pair - 1, 1)
            outs.append(jnp.dot(prod.astype(BF16), even, preferred_element_type=F32))
        return jnp.concatenate(outs, axis=-1).astype(BF16)

    _expert_groups(layer, start_ref, ntile_ref, total_ref, w_hbm, x_hbm, o_hbm, wbuf, wb_scr, xbuf, obuf, zbuf,
                   wsem, isem, osem, zsem, compute, n_chunks=n_chunks, n_groups=n_groups, n_tiles=n_tiles, tm=tm,
                   out_cols=nc // 2)


def _group_specs(layer, n_chunks, n_groups, nc):
    def b_map(g, *_):
        gg = jnp.minimum(g, n_groups - 1)
        return (layer, gg // n_chunks, 0, lax.rem(gg, n_chunks))
    any_spec = pl.BlockSpec(memory_space=pl.ANY)
    return [pl.BlockSpec((None, None, 1, nc), b_map), any_spec, any_spec]


def _moe_up(plan, xs, w_gate_up, b_gate_up, layer):
    p, half = xs.shape
    n_exp, d, f2 = w_gate_up.shape[1:]
    nc = min(MOE_CHUNK, f2)
    n_chunks = f2 // nc
    n_groups = n_exp * n_chunks
    tm = TM_MOE
    vmem = 2 * d * nc * 4 + d * nc * 2 + 2 * tm * half * 4 + 3 * tm * f2 + 8 * tm * nc * 4 + (6 << 20)
    grid_spec = pltpu.PrefetchScalarGridSpec(
        num_scalar_prefetch=3,
        grid=(n_groups + 1,),
        in_specs=_group_specs(layer, n_chunks, n_groups, nc),
        out_specs=pl.BlockSpec(memory_space=pl.ANY),
        scratch_shapes=[
            pltpu.VMEM((2, d, nc), F32),
            pltpu.VMEM((d, nc), BF16),
            pltpu.VMEM((2, tm, half), jnp.uint32),
            pltpu.VMEM((2, tm, nc // 2), BF16),
            pltpu.VMEM((tm, f2 // 2), BF16),
            pltpu.SemaphoreType.DMA((2,)), pltpu.SemaphoreType.DMA((2,)), pltpu.SemaphoreType.DMA((2,)),
            pltpu.SemaphoreType.DMA(()),
        ],
    )
    return pl.pallas_call(
        functools.partial(_moe_up_kernel, layer=layer, n_chunks=n_chunks, n_groups=n_groups, n_tiles=p // tm, tm=tm),
        grid_spec=grid_spec,
        out_shape=jax.ShapeDtypeStruct((p, f2 // 2), BF16),
        compiler_params=_params(1, vmem),
        name="moe_up",
    )(*plan, b_gate_up.reshape(b_gate_up.shape[0], n_exp, 1, f2), w_gate_up, xs)


def _moe_down_kernel(start_ref, ntile_ref, total_ref, b_ref, w_hbm, x_hbm, o_hbm,
                     wbuf, wb_scr, xbuf, obuf, zbuf, wsem, isem, osem, zsem, *, layer, n_chunks, n_groups, n_tiles, tm):
    def compute(a):
        return jnp.dot(a, wb_scr[...], preferred_element_type=F32) + b_ref[...]

    _expert_groups(layer, start_ref, ntile_ref, total_ref, w_hbm, x_hbm, o_hbm, wbuf, wb_scr, xbuf, obuf, zbuf,
                   wsem, isem, osem, zsem, compute, n_chunks=n_chunks, n_groups=n_groups, n_tiles=n_tiles, tm=tm,
                   out_cols=wb_scr.shape[-1])


def _moe_down(plan, act, w_down, b_down, layer):
    p, f = act.shape
    n_exp, _, d = w_down.shape[1:]
    nc = min(MOE_CHUNK, d)
    n_chunks = d // nc
    n_groups = n_exp * n_chunks
    tm = TM_MOE
    vmem = 2 * f * nc * 4 + f * nc * 2 + 2 * tm * f * 2 + tm * d * 4 + 5 * tm * nc * 4 + (6 << 20)
    grid_spec = pltpu.PrefetchScalarGridSpec(
        num_scalar_prefetch=3,
        grid=(n_groups + 1,),
        in_specs=_group_specs(layer, n_chunks, n_groups, nc),
        out_specs=pl.BlockSpec(memory_space=pl.ANY),
        scratch_shapes=[
            pltpu.VMEM((2, f, nc), F32),
            pltpu.VMEM((f, nc), BF16),
            pltpu.VMEM((2, tm, f), BF16),
            pltpu.VMEM((2, tm, nc), F32),
            pltpu.VMEM((tm, d), F32),
            pltpu.SemaphoreType.DMA((2,)), pltpu.SemaphoreType.DMA((2,)), pltpu.SemaphoreType.DMA((2,)),
            pltpu.SemaphoreType.DMA(()),
        ],
    )
    return pl.pallas_call(
        functools.partial(_moe_down_kernel, layer=layer, n_chunks=n_chunks, n_groups=n_groups, n_tiles=p // tm,
                          tm=tm),
        grid_spec=grid_spec,
        out_shape=jax.ShapeDtypeStruct((p, d), F32),
        compiler_params=_params(1, vmem),
        name="moe_down",
    )(*plan, b_down.reshape(b_down.shape[0], n_exp, 1, d), w_down, act)


def _moe_plan(counts):
    tiles_per = (counts + TM_MOE - 1) // TM_MOE
    tile_end = jnp.cumsum(tiles_per)
    tile_start = tile_end - tiles_per
    i32 = lambda a: a.astype(jnp.int32)
    return i32(tile_start), i32(tiles_per), i32(tile_end[-1:])


def _combine_kernel(pos_ref, ys_ref, x_ref, gate_ref, gf_ref, nw_ref, o_ref, buf, sem, *, tm, final):
    def issue(r, _):
        for kk in range(TOP_K):
            src = pos_ref[r * TOP_K + kk]
            pltpu.make_async_copy(ys_ref.at[pl.ds(src, 1)], buf.at[kk, pl.ds(r, 1)], sem).start()
        return ()

    lax.fori_loop(0, tm, issue, ())
    for kk in range(TOP_K):
        pltpu.make_async_copy(ys_ref.at[pl.ds(0, tm)], buf.at[kk], sem).wait()

    gates = gate_ref[...]
    y = gates[:, 0:1] * buf[0]
    for kk in range(1, TOP_K):
        y = y + gates[:, kk:kk + 1] * buf[kk]
    out = x_ref[...] + gf_ref[...] * y
    if final:
        out = _rms(out) * nw_ref[...]
    o_ref[...] = out


def _combine(pos_flat, ys, x1, gates, mod, layer, norm_final, final):
    s, d = x1.shape
    tm = min(TM_COMB, s)
    vmem = TOP_K * tm * d * 4 + 6 * tm * d * 4 + (6 << 20)
    return pl.pallas_call(
        functools.partial(_combine_kernel, tm=tm, final=final),
        grid=(s // tm,),
        in_specs=[
            pl.BlockSpec((tm * TOP_K,), lambda i: (i,), memory_space=pltpu.SMEM),
            pl.BlockSpec(memory_space=pl.ANY),
            pl.BlockSpec((tm, d), lambda i: (i, 0)),
            pl.BlockSpec((tm, V7X_LANES), lambda i: (i, 0)),
            pl.BlockSpec((None, 1, d), lambda i: (layer, 0, 5)),
            pl.BlockSpec((1, d), lambda i: (0, 0)),
        ],
        out_specs=pl.BlockSpec((tm, d), lambda i: (i, 0)),
        out_shape=jax.ShapeDtypeStruct((s, d), F32),
        scratch_shapes=[pltpu.VMEM((TOP_K, tm, d), F32), pltpu.SemaphoreType.DMA(())],
        compiler_params=_params(1, vmem),
        name="moe_combine",
    )(pos_flat, ys, x1, gates, mod, norm_final)


def kernel(x, c, positions, ada_w, ada_b, norm_mix, w_in, w_out, lambda_q1, lambda_k1, lambda_q2, lambda_k2,
           subln_w, sb_norm_w, norm_ffn, router_w, router_b, w_gate_up, b_gate_up, w_down, b_down, norm_final):
    batch, s, d = x.shape
    assert batch == 1, "the kernels are written for one sequence"
    depth = ada_w.shape[0]
    n_exp = router_w.shape[-1]
    assert n_exp <= V7X_LANES

    xc = x.reshape(s, d)
    mod = _ada_mod(c.reshape(d, 1), ada_w, ada_b)
    cos, sin = _rope_tables(positions.reshape(s, 1).astype(F32))
    w_in_b = w_in.astype(BF16)
    w_out_b = w_out.astype(BF16)

    n_tiles = (s * TOP_K) // TM_MOE + n_exp
    n_rows = n_tiles * TM_MOE
    pad = V7X_LANES - n_exp
    row = lambda a: a.reshape(1, -1)

    for l in range(depth):
        lam_init = 0.8 - 0.6 * math.exp(-0.3 * l)
        lam = (jnp.exp(jnp.sum(lambda_q1[l] * lambda_k1[l])) - jnp.exp(jnp.sum(lambda_q2[l] * lambda_k2[l]))
               + lam_init).reshape(1).astype(F32)

        qkv = _in_proj(xc, mod, l, row(norm_mix[l]), w_in_b, cos, sin)
        a_mix = _diff_attn(qkv, lam, row(subln_w[l]), 1.0 - lam_init)
        b_mix = _sb_attn(qkv, row(sb_norm_w[l]))

        rw = jnp.pad(router_w[l], ((0, 0), (0, pad)))
        rw_hi = rw.astype(BF16)
        rw_lo = (rw - rw_hi.astype(F32)).astype(BF16)
        rb = jnp.pad(router_b[l], (0, pad), constant_values=NEG_BIG).reshape(1, V7X_LANES)
        x1, hp, eidx, gates, sel = _out_router(a_mix, b_mix, w_out_b, xc, mod, l, row(norm_ffn[l]),
                                               rw_hi, rw_lo, rb)

        rank, counts = _rank(sel)
        counts = counts[0, :n_exp].astype(jnp.int32)
        plan = _moe_plan(counts)
        slot = rank[:, :n_exp] + (plan[0] * TM_MOE).astype(F32)[None, :]
        pos = jnp.take_along_axis(slot, eidx[:, :TOP_K], axis=1).astype(jnp.int32).reshape(-1)

        xs = _scatter_rows(pos, hp, jnp.zeros((n_rows, d // 2), jnp.uint32))
        act = _moe_up(plan, xs, w_gate_up, b_gate_up, l)
        ys = _moe_down(plan, act, w_down, b_down, l)
        xc = _combine(pos, ys, x1, gates, mod, l, row(norm_final), l == depth - 1)

    return xc.reshape(batch, s, d)
```

```python
import functools
import math

import jax
import jax.numpy as jnp
from jax import lax
from jax.experimental import pallas as pl
from jax.experimental.pallas import tpu as pltpu

F32 = jnp.float32
BF16 = jnp.bfloat16

HEAD_DIM = 128
DIFF_QK_DIM = HEAD_DIM // 2
N_MOD = 6
TOP_K = 4
SWIGLU_LIMIT = 7.0
SWIGLU_ALPHA = 1.702
ROPE_THETA = 10000.0
RMS_EPS = 1e-5

V7X_LANES = 128
V7X_VMEM_BYTES = 64 * 1024 * 1024

TM_PROJ = 512
TQ_ATTN = 256
TQ_DIFF = 512
SB_HEADS_PER_STEP = 4
DIFF_HEADS_PER_STEP = 2
TM_OUT = 256
TM_RANK = 256
TM_MOE = 256
TM_COMB = 256
MOE_CHUNK = 1024
MOD_CHUNK = 1024
X_AHEAD = 2
W_SLABS = 4
NEG_BIG = -1e30
SB_EXIT_LOG = -100.0

_NT_DIMS = (((1,), (1,)), ((), ()))


def _params(n_axes, vmem_bytes):
    return pltpu.CompilerParams(
        dimension_semantics=("arbitrary",) * n_axes,
        vmem_limit_bytes=min(int(vmem_bytes), V7X_VMEM_BYTES - 8 * 1024 * 1024),
    )


def _rms(x):
    return x * lax.rsqrt(jnp.mean(x * x, axis=-1, keepdims=True) + RMS_EPS)


def _mod_kernel(c_ref, w_ref, b_ref, o_ref):
    c = c_ref[...]
    ca = c * (1.0 / (1.0 + jnp.exp(-c)))
    o_ref[...] = jnp.sum(ca * w_ref[...], axis=0, keepdims=True) + b_ref[...]


def _ada_mod(c_col, ada_w, ada_b):
    depth, d, n = ada_w.shape
    nc = min(MOD_CHUNK, n)
    return pl.pallas_call(
        _mod_kernel,
        grid=(depth, n // nc),
        in_specs=[
            pl.BlockSpec((d, 1), lambda l, j: (0, 0)),
            pl.BlockSpec((None, d, nc), lambda l, j: (l, 0, j)),
            pl.BlockSpec((None, 1, nc), lambda l, j: (l, 0, j)),
        ],
        out_specs=pl.BlockSpec((None, 1, nc), lambda l, j: (l, 0, j)),
        out_shape=jax.ShapeDtypeStruct((depth, 1, n), F32),
        compiler_params=_params(2, 3 * d * nc * 4 + 4 * d * V7X_LANES * 4),
        name="ada_mod",
    )(c_col, ada_w, ada_b.reshape(depth, 1, n))


def _rope_kernel(pos_ref, invf_ref, sign_ref, cos_ref, sin_ref):
    ang = pos_ref[...] * invf_ref[...]
    cos_ref[...] = jnp.cos(ang)
    sin_ref[...] = jnp.sin(ang) * sign_ref[...]


def _rope_tables(pos_col):
    s = pos_col.shape[0]
    half = DIFF_QK_DIM // 2
    inv_freq = ROPE_THETA ** (-jnp.arange(half, dtype=F32) / half)
    reps = V7X_LANES // half
    invf = jnp.tile(inv_freq, reps).reshape(1, V7X_LANES)
    sign = jnp.tile(jnp.concatenate([-jnp.ones((half,), F32), jnp.ones((half,), F32)]), reps // 2)
    ts = min(1024, s)
    tab = jax.ShapeDtypeStruct((s, V7X_LANES), F32)
    return pl.pallas_call(
        _rope_kernel,
        grid=(s // ts,),
        in_specs=[
            pl.BlockSpec((ts, 1), lambda i: (i, 0)),
            pl.BlockSpec((1, V7X_LANES), lambda i: (0, 0)),
            pl.BlockSpec((1, V7X_LANES), lambda i: (0, 0)),
        ],
        out_specs=[pl.BlockSpec((ts, V7X_LANES), lambda i: (i, 0))] * 2,
        out_shape=[tab, tab],
        compiler_params=_params(1, 32 * 1024 * 1024),
        name="rope_tables",
    )(pos_col, invf, sign.reshape(1, V7X_LANES))


def _in_proj_kernel(x_ref, nw_ref, sc_ref, sh_ref, w_ref, cos_ref, sin_ref, o_ref, h_scr, *, n_heads):
    j = pl.program_id(1)

    @pl.when(j == 0)
    def _():
        h = _rms(x_ref[...]) * nw_ref[...]
        h_scr[...] = (h * (1.0 + sc_ref[...]) + sh_ref[...]).astype(BF16)

    res = jnp.dot(h_scr[...], w_ref[...], preferred_element_type=F32)

    def store(scale, rotary):
        if rotary:
            cos = cos_ref[...]
            sin = sin_ref[...]
            lane = lax.broadcasted_iota(jnp.int32, cos.shape, 1)
            first_half = (lane % DIFF_QK_DIM) < (DIFF_QK_DIM // 2)
        for g in range(n_heads):
            blk = res[:, g * HEAD_DIM:(g + 1) * HEAD_DIM]
            if rotary:
                fwd = pltpu.roll(blk, HEAD_DIM - DIFF_QK_DIM // 2, 1)
                bwd = pltpu.roll(blk, DIFF_QK_DIM // 2, 1)
                blk = blk * cos + jnp.where(first_half, fwd, bwd) * sin
            if scale != 1.0:
                blk = blk * scale
            o_ref[g] = blk.astype(BF16)

    pl.when(j == 0)(lambda: store(DIFF_QK_DIM ** -0.5 * math.log2(math.e), True))
    pl.when(j == 1)(lambda: store(1.0, True))
    pl.when(j == 3)(lambda: store(HEAD_DIM ** -0.5, False))
    pl.when((j == 2) | (j >= 4))(lambda: store(1.0, False))


def _in_proj(x, mod, layer, norm_w, w_in, cos, sin):
    s, d = x.shape
    gw = w_in.shape[-1] // 6
    n_heads = gw // HEAD_DIM
    tm = min(TM_PROJ, s)
    vmem = 2 * tm * d * 4 + tm * d * 2 + 2 * d * gw * 2 + 2 * tm * gw * 2 + 3 * tm * gw * 4 + (4 << 20)
    return pl.pallas_call(
        functools.partial(_in_proj_kernel, n_heads=n_heads),
        grid=(s // tm, 6),
        in_specs=[
            pl.BlockSpec((tm, d), lambda i, j: (i, 0)),
            pl.BlockSpec((1, d), lambda i, j: (0, 0)),
            pl.BlockSpec((None, 1, d), lambda i, j: (layer, 0, 1)),
            pl.BlockSpec((None, 1, d), lambda i, j: (layer, 0, 0)),
            pl.BlockSpec((None, d, gw), lambda i, j: (layer, 0, j)),
            pl.BlockSpec((tm, V7X_LANES), lambda i, j: (i, 0)),
            pl.BlockSpec((tm, V7X_LANES), lambda i, j: (i, 0)),
        ],
        out_specs=pl.BlockSpec((n_heads, tm, HEAD_DIM), lambda i, j: (j, i, 0)),
        out_shape=jax.ShapeDtypeStruct((6 * n_heads, s, HEAD_DIM), BF16),
        scratch_shapes=[pltpu.VMEM((tm, d), BF16)],
        compiler_params=_params(2, vmem),
        name="in_proj",
    )(x, norm_w, mod, mod, w_in, cos, sin)


def _diff_attn_kernel(lam_ref, q_ref, k_ref, v_ref, w_ref, o_ref, m_scr, l_scr, acc_scr, *, tq, out_scale, n_group):
    qi = pl.program_id(1)
    lane = lax.broadcasted_iota(jnp.int32, (tq, HEAD_DIM), 1)
    n_chunks = tq // V7X_LANES
    m_scr[...] = jnp.full(m_scr.shape, NEG_BIG, F32)
    l_scr[...] = jnp.zeros(l_scr.shape, F32)
    acc_scr[...] = jnp.zeros(acc_scr.shape, F32)

    def step(ki, diag):
        start = pl.multiple_of(ki * tq, tq)
        for h in range(n_group):
            q = q_ref[h]
            zero = jnp.zeros_like(q)
            k = k_ref[h, pl.ds(start, tq), :]
            v = v_ref[h, pl.ds(start, tq), :]
            for p in range(2):
                q_sub = jnp.where((lane < DIFF_QK_DIM) == (p == 0), q, zero)
                s = lax.dot_general(q_sub, k, _NT_DIMS, preferred_element_type=F32)
                if diag:
                    row = lax.broadcasted_iota(jnp.int32, s.shape, 0)
                    col = lax.broadcasted_iota(jnp.int32, s.shape, 1)
                    s = jnp.where(col <= row, s, NEG_BIG)
                chunks = [s[:, c * V7X_LANES:(c + 1) * V7X_LANES] for c in range(n_chunks)]
                i = 2 * h + p
                m = m_scr[i]
                m_new = jnp.maximum(m, jnp.max(functools.reduce(jnp.maximum, chunks), axis=-1, keepdims=True))
                alpha = jnp.exp2(m - m_new)
                pe = [jnp.exp2(ch - m_new) for ch in chunks]
                l_scr[i] = alpha * l_scr[i] + functools.reduce(jnp.add, pe)
                pb = jnp.concatenate([x.astype(BF16) for x in pe], axis=-1)
                acc_scr[i] = alpha * acc_scr[i] + jnp.dot(pb, v, preferred_element_type=F32)
                m_scr[i] = m_new

    def body(ki, c):
        step(ki, False)
        return c

    lax.fori_loop(0, qi, body, 0)
    step(qi, True)
    for h in range(n_group):
        den = [jnp.sum(l_scr[2 * h + p], axis=-1, keepdims=True) for p in range(2)]
        a = acc_scr[2 * h] / den[0] - lam_ref[0] * (acc_scr[2 * h + 1] / den[1])
        o_ref[h] = (_rms(a) * (w_ref[...] * out_scale)).astype(BF16)


def _diff_attn(qkv, lam, subln_w, out_scale):
    n_heads = qkv.shape[0] // 6
    s = qkv.shape[1]
    tq = min(TQ_DIFF, s)
    n_group = math.gcd(DIFF_HEADS_PER_STEP, n_heads)
    vmem = 4 * n_group * s * HEAD_DIM * 2 + n_group * (16 * tq * tq * 4 + 6 * tq * V7X_LANES * 4) + (8 << 20)
    kv = lambda base: (lambda hg, i: (base * n_heads // n_group + hg, 0, 0))
    return pl.pallas_call(
        functools.partial(_diff_attn_kernel, tq=tq, out_scale=out_scale, n_group=n_group),
        grid=(n_heads // n_group, s // tq),
        in_specs=[
            pl.BlockSpec(memory_space=pltpu.SMEM),
            pl.BlockSpec((n_group, tq, HEAD_DIM), lambda hg, i: (hg, i, 0)),
            pl.BlockSpec((n_group, s, HEAD_DIM), kv(1)),
            pl.BlockSpec((n_group, s, HEAD_DIM), kv(2)),
            pl.BlockSpec((1, HEAD_DIM), lambda hg, i: (0, 0)),
        ],
        out_specs=pl.BlockSpec((n_group, tq, HEAD_DIM), lambda hg, i: (hg, i, 0)),
        out_shape=jax.ShapeDtypeStruct((n_heads, s, HEAD_DIM), BF16),
        scratch_shapes=[pltpu.VMEM((2 * n_group, tq, V7X_LANES), F32), pltpu.VMEM((2 * n_group, tq, V7X_LANES), F32),
                        pltpu.VMEM((2 * n_group, tq, HEAD_DIM), F32)],
        compiler_params=_params(2, vmem),
        name="diff_attn",
    )(lam, qkv, qkv, qkv, subln_w)


def _sb_attn_kernel(q_ref, k_ref, v_ref, w_ref, o_ref, *, tq, n_group):
    qi = pl.program_id(1)
    n_chunks = tq // V7X_LANES
    row = lax.broadcasted_iota(jnp.int32, (tq, tq), 0)
    col = lax.broadcasted_iota(jnp.int32, (tq, tq), 1)
    strict = col < row
    after = jnp.where(row > col, 1.0, 0.0).astype(BF16)
    ones = jnp.ones((tq, V7X_LANES), BF16)

    def block(h, ki, rem, acc, diag):
        start = pl.multiple_of(ki * tq, tq)
        k = k_ref[h, pl.ds(start, tq), :]
        v = v_ref[h, pl.ds(start, tq), :]
        z = lax.dot_general(q_ref[h], k, _NT_DIMS, preferred_element_type=F32)
        t = jnp.log1p(jnp.exp(-jnp.abs(z)))
        log_beta = jnp.minimum(z, 0.0) - t
        log_1m = -jnp.maximum(z, 0.0) - t
        if diag:
            log_1m = jnp.where(strict, log_1m, 0.0)
        hi = log_1m.astype(BF16)
        lo = (log_1m - hi.astype(F32)).astype(BF16)
        later = (jnp.dot(hi, after, preferred_element_type=F32)
                 + jnp.dot(lo, after, preferred_element_type=F32))
        total = (jnp.dot(hi, ones, preferred_element_type=F32)
                 + jnp.dot(lo, ones, preferred_element_type=F32))
        log_a = log_beta + later
        a = []
        for c in range(n_chunks):
            sl = slice(c * V7X_LANES, (c + 1) * V7X_LANES)
            a_c = jnp.exp(log_a[:, sl] + rem)
            if diag:
                a_c = jnp.where(strict[:, sl], a_c, 0.0)
            a.append(a_c.astype(BF16))
        acc = acc + jnp.dot(jnp.concatenate(a, axis=-1), v, preferred_element_type=F32)
        return rem + total, acc

    zero = jnp.zeros((tq, V7X_LANES), F32)
    state = [block(h, qi, zero, zero, True) for h in range(n_group)]
    rems = tuple(s[0] for s in state)
    accs = tuple(s[1] for s in state)

    def cond(c):
        ki, rems, _ = c
        return jnp.logical_and(ki >= 0, jnp.max(functools.reduce(jnp.maximum, rems)) > SB_EXIT_LOG)

    def body(c):
        ki, rems, accs = c
        state = [block(h, ki, rems[h], accs[h], False) for h in range(n_group)]
        return ki - 1, tuple(s[0] for s in state), tuple(s[1] for s in state)

    _, _, accs = lax.while_loop(cond, body, (qi - 1, rems, accs))
    for h in range(n_group):
        o_ref[h] = (_rms(accs[h]) * w_ref[...]).astype(BF16)


def _sb_attn(qkv, norm_w):
    n_heads = qkv.shape[0] // 6
    s = qkv.shape[1]
    tq = min(TQ_ATTN, s)
    n_group = math.gcd(SB_HEADS_PER_STEP, n_heads)
    vmem = 4 * n_group * s * HEAD_DIM * 2 + n_group * 12 * tq * tq * 4 + (8 << 20)
    blk = lambda base: (lambda hg, i: (base * n_heads // n_group + hg, 0, 0))
    return pl.pallas_call(
        functools.partial(_sb_attn_kernel, tq=tq, n_group=n_group),
        grid=(n_heads // n_group, s // tq),
        in_specs=[
            pl.BlockSpec((n_group, tq, HEAD_DIM), lambda hg, i: (3 * n_heads // n_group + hg, i, 0)),
            pl.BlockSpec((n_group, s, HEAD_DIM), blk(4)),
            pl.BlockSpec((n_group, s, HEAD_DIM), blk(5)),
            pl.BlockSpec((1, HEAD_DIM), lambda hg, i: (0, 0)),
        ],
        out_specs=pl.BlockSpec((n_group, tq, HEAD_DIM), lambda hg, i: (hg, i, 0)),
        out_shape=jax.ShapeDtypeStruct((n_heads, s, HEAD_DIM), BF16),
        compiler_params=_params(2, vmem),
        name="sb_attn",
    )(qkv, qkv, qkv, norm_w)


def _out_router_kernel(a_ref, b_ref, w_ref, x_ref, ga_ref, nw_ref, sc_ref, sh_ref, rwh_ref, rwl_ref, rb_ref,
                       x1_ref, hp_ref, eidx_ref, gate_ref, sel_ref, *, n_heads):
    mix = jnp.concatenate([a_ref[g] for g in range(n_heads)] + [b_ref[g] for g in range(n_heads)], axis=-1)
    y = jnp.dot(mix, w_ref[...], preferred_element_type=F32)
    x1 = x_ref[...] + ga_ref[...] * y
    x1_ref[...] = x1
    h = _rms(x1) * nw_ref[...]
    h = h * (1.0 + sc_ref[...]) + sh_ref[...]
    half = h.shape[-1] // 2
    hp_ref[...] = pltpu.pack_elementwise([h[:, :half], h[:, half:]], packed_dtype=BF16)

    h_hi = h.astype(BF16)
    h_lo = (h - h_hi.astype(F32)).astype(BF16)
    logits = (jnp.dot(h_hi, rwh_ref[...], preferred_element_type=F32)
              + jnp.dot(h_lo, rwh_ref[...], preferred_element_type=F32)
              + jnp.dot(h_hi, rwl_ref[...], preferred_element_type=F32)) + rb_ref[...]

    lane = lax.broadcasted_iota(jnp.int32, logits.shape, 1).astype(F32)
    vals, idxs = [], []
    sel = jnp.zeros(logits.shape, F32)
    for _ in range(TOP_K):
        m = jnp.max(logits, axis=-1, keepdims=True)
        idx = jnp.min(jnp.where(logits == m, lane, float(V7X_LANES)), axis=-1, keepdims=True)
        hit = lane == idx
        vals.append(m)
        idxs.append(idx)
        sel = jnp.where(hit, 1.0, sel)
        logits = jnp.where(hit, -3e38, logits)
    exps = [jnp.exp(v - vals[0]) for v in vals]
    denom = exps[0] + exps[1] + exps[2] + exps[3]
    eidx = jnp.zeros(logits.shape, F32)
    gate = jnp.zeros(logits.shape, F32)
    for kk in range(TOP_K):
        eidx = jnp.where(lane == float(kk), idxs[kk], eidx)
        gate = jnp.where(lane == float(kk), exps[kk] / denom, gate)
    eidx_ref[...] = eidx.astype(jnp.int32)
    gate_ref[...] = gate
    sel_ref[...] = sel.astype(BF16)


def _out_router(a_mix, b_mix, w_out, x, mod, layer, norm_w, rw_hi, rw_lo, rb):
    n_heads, s, _ = a_mix.shape
    d = x.shape[1]
    tm = min(TM_OUT, s)
    vmem = 2 * d * d * 2 + 10 * tm * d * 4 + (8 << 20)
    row = lambda i: (i, 0)
    fixed = lambda i: (0, 0)
    lanes = jax.ShapeDtypeStruct((s, V7X_LANES), F32)
    return pl.pallas_call(
        functools.partial(_out_router_kernel, n_heads=n_heads),
        grid=(s // tm,),
        in_specs=[
            pl.BlockSpec((n_heads, tm, HEAD_DIM), lambda i: (0, i, 0)),
            pl.BlockSpec((n_heads, tm, HEAD_DIM), lambda i: (0, i, 0)),
            pl.BlockSpec((None, d, d), lambda i: (layer, 0, 0)),
            pl.BlockSpec((tm, d), row),
            pl.BlockSpec((None, 1, d), lambda i: (layer, 0, 2)),
            pl.BlockSpec((1, d), fixed),
            pl.BlockSpec((None, 1, d), lambda i: (layer, 0, 4)),
            pl.BlockSpec((None, 1, d), lambda i: (layer, 0, 3)),
            pl.BlockSpec((d, V7X_LANES), fixed),
            pl.BlockSpec((d, V7X_LANES), fixed),
            pl.BlockSpec((1, V7X_LANES), fixed),
        ],
        out_specs=[
            pl.BlockSpec((tm, d), row),
            pl.BlockSpec((tm, d // 2), row),
            pl.BlockSpec((tm, V7X_LANES), row),
            pl.BlockSpec((tm, V7X_LANES), row),
            pl.BlockSpec((tm, V7X_LANES), row),
        ],
        out_shape=[
            jax.ShapeDtypeStruct((s, d), F32),
            jax.ShapeDtypeStruct((s, d // 2), jnp.uint32),
            jax.ShapeDtypeStruct((s, V7X_LANES), jnp.int32),
            lanes,
            jax.ShapeDtypeStruct((s, V7X_LANES), BF16),
        ],
        compiler_params=_params(1, vmem),
        name="out_router",
    )(a_mix, b_mix, w_out, x, mod, norm_w, mod, mod, rw_hi, rw_lo, rb)


def _rank_kernel(sel_ref, rank_ref, count_ref, run_scr):
    i = pl.program_id(0)

    @pl.when(i == 0)
    def _():
        run_scr[...] = jnp.zeros_like(run_scr)

    sel = sel_ref[...]
    tm = sel.shape[0]
    row = lax.broadcasted_iota(jnp.int32, (tm, tm), 0)
    col = lax.broadcasted_iota(jnp.int32, (tm, tm), 1)
    before = jnp.where(col < row, 1.0, 0.0).astype(BF16)
    run = run_scr[...]
    rank_ref[...] = jnp.dot(before, sel, preferred_element_type=F32) + run
    run = run + jnp.sum(sel.astype(F32), axis=0, keepdims=True)
    run_scr[...] = run
    count_ref[...] = run


def _rank(sel):
    s = sel.shape[0]
    tm = min(TM_RANK, s)
    return pl.pallas_call(
        _rank_kernel,
        grid=(s // tm,),
        in_specs=[pl.BlockSpec((tm, V7X_LANES), lambda i: (i, 0))],
        out_specs=[pl.BlockSpec((tm, V7X_LANES), lambda i: (i, 0)),
                   pl.BlockSpec((1, V7X_LANES), lambda i: (0, 0))],
        out_shape=[jax.ShapeDtypeStruct((s, V7X_LANES), F32), jax.ShapeDtypeStruct((1, V7X_LANES), F32)],
        scratch_shapes=[pltpu.VMEM((1, V7X_LANES), F32)],
        compiler_params=_params(1, 16 * 1024 * 1024),
        name="route_rank",
    )(sel)


def _scatter_kernel(pos_ref, h_ref, xs_in_ref, xs_ref, sem, *, tm):
    del xs_in_ref

    def issue(r, _):
        for kk in range(TOP_K):
            dst = pos_ref[r * TOP_K + kk]
            pltpu.make_async_copy(h_ref.at[pl.ds(r, 1)], xs_ref.at[pl.ds(dst, 1)], sem).start()
        return ()

    lax.fori_loop(0, tm, issue, ())
    for _ in range(TOP_K):
        pltpu.make_async_copy(h_ref, xs_ref.at[pl.ds(0, tm)], sem).wait()


def _scatter_rows(pos_flat, hp, xs_zero):
    s, w = hp.shape
    tm = min(TM_COMB, s)
    return pl.pallas_call(
        functools.partial(_scatter_kernel, tm=tm),
        grid=(s // tm,),
        in_specs=[
            pl.BlockSpec((tm * TOP_K,), lambda i: (i,), memory_space=pltpu.SMEM),
            pl.BlockSpec((tm, w), lambda i: (i, 0)),
            pl.BlockSpec(memory_space=pl.ANY),
        ],
        out_specs=pl.BlockSpec(memory_space=pl.ANY),
        out_shape=jax.ShapeDtypeStruct(xs_zero.shape, xs_zero.dtype),
        scratch_shapes=[pltpu.SemaphoreType.DMA(())],
        input_output_aliases={2: 0},
        compiler_params=_params(1, 16 * 1024 * 1024),
        name="moe_scatter",
    )(pos_flat, hp, xs_zero)


def _row_tiles(n, t0, col0, x_hbm, o_hbm, xbuf, obuf, isem, osem, compute, before_loop, each_tile, *, tm, out_cols):
    def x_copy(t, slot):
        r0 = pl.multiple_of((t0 + t) * tm, tm)
        return pltpu.make_async_copy(x_hbm.at[pl.ds(r0, tm)], xbuf.at[slot], isem.at[slot])

    def o_copy(t, slot):
        r0 = pl.multiple_of((t0 + t) * tm, tm)
        return pltpu.make_async_copy(obuf.at[slot], o_hbm.at[pl.ds(r0, tm), pl.ds(col0, out_cols)], osem.at[slot])

    for a in range(X_AHEAD):
        @pl.when(n > a)
        def _():
            x_copy(a, a).start()

    before_loop()

    def body(t, carry):
        slot = lax.rem(t, 2)

        @pl.when(t + X_AHEAD < n)
        def _():
            x_copy(t + X_AHEAD, lax.rem(t + X_AHEAD, X_AHEAD + 1)).start()

        each_tile(t)
        xslot = lax.rem(t, X_AHEAD + 1)
        x_copy(t, xslot).wait()

        @pl.when(t >= 2)
        def _():
            o_copy(t - 2, slot).wait()

        obuf[slot] = compute(xbuf[xslot])
        o_copy(t, slot).start()
        return carry

    lax.fori_loop(0, n, body, 0)

    @pl.when(n >= 2)
    def _():
        o_copy(n - 2, lax.rem(n, 2)).wait()

    @pl.when(n >= 1)
    def _():
        o_copy(n - 1, lax.rem(n - 1, 2)).wait()


def _zero_fill_rows(first_tile, n_tiles, o_hbm, zbuf, zsem, *, tm):
    zbuf[...] = jnp.zeros_like(zbuf)

    def z_copy(t):
        r0 = pl.multiple_of(t * tm, tm)
        return pltpu.make_async_copy(zbuf, o_hbm.at[pl.ds(r0, tm)], zsem)

    def start(t, carry):
        z_copy(t).start()
        return carry

    def wait(t, carry):
        z_copy(t).wait()
        return carry

    lax.fori_loop(first_tile, n_tiles, start, 0)
    lax.fori_loop(first_tile, n_tiles, wait, 0)


def _expert_groups(layer, start_ref, ntile_ref, total_ref, w_hbm, x_hbm, o_hbm, wbuf, wb_scr, xbuf, obuf, zbuf,
                   wsem, isem, osem, zsem, compute, *, n_chunks, n_groups, n_tiles, tm, out_cols):
    g = pl.program_id(0)
    d_in, nc = wb_scr.shape
    rows = d_in // W_SLABS

    def tiles_of(gg):
        return ntile_ref[gg // n_chunks]

    def w_copy(gg, k):
        col = pl.multiple_of(lax.rem(gg, n_chunks) * nc, nc)
        r0 = pl.multiple_of(k * rows, rows)
        slot = lax.rem(gg, 2)
        return pltpu.make_async_copy(w_hbm.at[layer, gg // n_chunks, pl.ds(r0, rows), pl.ds(col, nc)],
                                     wbuf.at[slot, pl.ds(r0, rows)], wsem.at[slot])

    @pl.when(jnp.logical_and(g == 0, tiles_of(0) > 0))
    def _():
        for k in range(W_SLABS):
            w_copy(0, k).start()

    @pl.when(g < n_groups)
    def _():
        n = tiles_of(g)
        nxt = jnp.minimum(g + 1, n_groups - 1)
        fetch_next = jnp.logical_and(g + 1 < n_groups, tiles_of(nxt) > 0)
        col0 = pl.multiple_of(lax.rem(g, n_chunks) * out_cols, out_cols)

        def before_loop():
            @pl.when(n > 0)
            def _():
                for k in range(W_SLABS):
                    w_copy(g, k).wait()
                wb_scr[...] = wbuf[lax.rem(g, 2)].astype(BF16)

        def each_tile(t):
            @pl.when(jnp.logical_and(fetch_next, t < W_SLABS))
            def _():
                w_copy(nxt, t).start()

        _row_tiles(n, start_ref[g // n_chunks], col0, x_hbm, o_hbm, xbuf, obuf, isem, osem, compute,
                   before_loop, each_tile, tm=tm, out_cols=out_cols)

        for k in range(W_SLABS):
            @pl.when(jnp.logical_and(fetch_next, n <= k))
            def _():
                w_copy(nxt, k).start()

    @pl.when(g == n_groups)
    def _():
        _zero_fill_rows(total_ref[0], n_tiles, o_hbm, zbuf, zsem, tm=tm)


def _moe_up_kernel(start_ref, ntile_ref, total_ref, b_ref, w_hbm, x_hbm, o_hbm,
                   wbuf, wb_scr, xbuf, obuf, zbuf, wsem, isem, osem, zsem, *, layer, n_chunks, n_groups, n_tiles, tm):
    nc = wb_scr.shape[-1]

    def compute(xp):
        half = xp.shape[-1]
        x_lo = pltpu.unpack_elementwise(xp, index=0, packed_dtype=BF16, unpacked_dtype=F32).astype(BF16)
        x_hi = pltpu.unpack_elementwise(xp, index=1, packed_dtype=BF16, unpacked_dtype=F32).astype(BF16)
        gu = (jnp.dot(x_lo, wb_scr[0:half, :], preferred_element_type=F32)
              + jnp.dot(x_hi, wb_scr[half:2 * half, :], preferred_element_type=F32)) + b_ref[...]
        gate = jnp.minimum(gu, SWIGLU_LIMIT)
        glu = gate * (1.0 / (1.0 + jnp.exp(-SWIGLU_ALPHA * gate)))
        up1 = jnp.clip(gu, -SWIGLU_LIMIT, SWIGLU_LIMIT) + 1.0
        pair = 2 * V7X_LANES
        r = lax.broadcasted_iota(jnp.int32, (pair, V7X_LANES), 0)
        c = lax.broadcasted_iota(jnp.int32, (pair, V7X_LANES), 1)
        even = jnp.where(r == 2 * c, 1.0, 0.0).astype(BF16)
        outs = []
        for q in range(gu.shape[-1] // pair):
            sl = slice(q * pair, (q + 1) * pair)
            prod = glu[:, sl] * pltpu.roll(up1[:, sl], pair - 1, 1)
            outs.append(jnp.dot(prod.astype(BF16), even, preferred_element_type=F32))
        return jnp.concatenate(outs, axis=-1).astype(BF16)

    _expert_groups(layer, start_ref, ntile_ref, total_ref, w_hbm, x_hbm, o_hbm, wbuf, wb_scr, xbuf, obuf, zbuf,
                   wsem, isem, osem, zsem, compute, n_chunks=n_chunks, n_groups=n_groups, n_tiles=n_tiles, tm=tm,
                   out_cols=nc // 2)


def _group_specs(layer, n_chunks, n_groups, nc):
    def b_map(g, *_):
        gg = jnp.minimum(g, n_groups - 1)
        return (layer, gg // n_chunks, 0, lax.rem(gg, n_chunks))
    any_spec = pl.BlockSpec(memory_space=pl.ANY)
    return [pl.BlockSpec((None, None, 1, nc), b_map), any_spec, any_spec]


def _moe_up(plan, xs, w_gate_up, b_gate_up, layer):
    p, half = xs.shape
    n_exp, d, f2 = w_gate_up.shape[1:]
    nc = min(MOE_CHUNK, f2)
    n_chunks = f2 // nc
    n_groups = n_exp * n_chunks
    tm = TM_MOE
    vmem = 2 * d * nc * 4 + d * nc * 2 + 2 * tm * half * 4 + 3 * tm * f2 + 8 * tm * nc * 4 + (6 << 20)
    grid_spec = pltpu.PrefetchScalarGridSpec(
        num_scalar_prefetch=3,
        grid=(n_groups + 1,),
        in_specs=_group_specs(layer, n_chunks, n_groups, nc),
        out_specs=pl.BlockSpec(memory_space=pl.ANY),
        scratch_shapes=[
            pltpu.VMEM((2, d, nc), F32),
            pltpu.VMEM((d, nc), BF16),
            pltpu.VMEM((X_AHEAD + 1, tm, half), jnp.uint32),
            pltpu.VMEM((2, tm, nc // 2), BF16),
            pltpu.VMEM((tm, f2 // 2), BF16),
            pltpu.SemaphoreType.DMA((2,)), pltpu.SemaphoreType.DMA((X_AHEAD + 1,)), pltpu.SemaphoreType.DMA((2,)),
            pltpu.SemaphoreType.DMA(()),
        ],
    )
    return pl.pallas_call(
        functools.partial(_moe_up_kernel, layer=layer, n_chunks=n_chunks, n_groups=n_groups, n_tiles=p // tm, tm=tm),
        grid_spec=grid_spec,
        out_shape=jax.ShapeDtypeStruct((p, f2 // 2), BF16),
        compiler_params=_params(1, vmem),
        name="moe_up",
    )(*plan, b_gate_up.reshape(b_gate_up.shape[0], n_exp, 1, f2), w_gate_up, xs)


def _moe_down_kernel(start_ref, ntile_ref, total_ref, b_ref, w_hbm, x_hbm, o_hbm,
                     wbuf, wb_scr, xbuf, obuf, zbuf, wsem, isem, osem, zsem, *, layer, n_chunks, n_groups, n_tiles, tm):
    def compute(a):
        return jnp.dot(a, wb_scr[...], preferred_element_type=F32) + b_ref[...]

    _expert_groups(layer, start_ref, ntile_ref, total_ref, w_hbm, x_hbm, o_hbm, wbuf, wb_scr, xbuf, obuf, zbuf,
                   wsem, isem, osem, zsem, compute, n_chunks=n_chunks, n_groups=n_groups, n_tiles=n_tiles, tm=tm,
                   out_cols=wb_scr.shape[-1])


def _moe_down(plan, act, w_down, b_down, layer):
    p, f = act.shape
    n_exp, _, d = w_down.shape[1:]
    nc = min(MOE_CHUNK, d)
    n_chunks = d // nc
    n_groups = n_exp * n_chunks
    tm = TM_MOE
    vmem = 2 * f * nc * 4 + f * nc * 2 + 2 * tm * f * 2 + tm * d * 4 + 5 * tm * nc * 4 + (6 << 20)
    grid_spec = pltpu.PrefetchScalarGridSpec(
        num_scalar_prefetch=3,
        grid=(n_groups + 1,),
        in_specs=_group_specs(layer, n_chunks, n_groups, nc),
        out_specs=pl.BlockSpec(memory_space=pl.ANY),
        scratch_shapes=[
            pltpu.VMEM((2, f, nc), F32),
            pltpu.VMEM((f, nc), BF16),
            pltpu.VMEM((X_AHEAD + 1, tm, f), BF16),
            pltpu.VMEM((2, tm, nc), F32),
            pltpu.VMEM((tm, d), F32),
            pltpu.SemaphoreType.DMA((2,)), pltpu.SemaphoreType.DMA((X_AHEAD + 1,)), pltpu.SemaphoreType.DMA((2,)),
            pltpu.SemaphoreType.DMA(()),
        ],
    )
    return pl.pallas_call(
        functools.partial(_moe_down_kernel, layer=layer, n_chunks=n_chunks, n_groups=n_groups, n_tiles=p // tm,
                          tm=tm),
        grid_spec=grid_spec,
        out_shape=jax.ShapeDtypeStruct((p, d), F32),
        compiler_params=_params(1, vmem),
        name="moe_down",
    )(*plan, b_down.reshape(b_down.shape[0], n_exp, 1, d), w_down, act)


def _moe_plan(counts):
    tiles_per = (counts + TM_MOE - 1) // TM_MOE
    tile_end = jnp.cumsum(tiles_per)
    tile_start = tile_end - tiles_per
    i32 = lambda a: a.astype(jnp.int32)
    return i32(tile_start), i32(tiles_per), i32(tile_end[-1:])


def _combine_kernel(pos_ref, ys_ref, x_ref, gate_ref, gf_ref, nw_ref, o_ref, buf, sem, *, tm, final):
    def issue(r, _):
        for kk in range(TOP_K):
            src = pos_ref[r * TOP_K + kk]
            pltpu.make_async_copy(ys_ref.at[pl.ds(src, 1)], buf.at[kk, pl.ds(r, 1)], sem).start()
        return ()

    lax.fori_loop(0, tm, issue, ())
    for kk in range(TOP_K):
        pltpu.make_async_copy(ys_ref.at[pl.ds(0, tm)], buf.at[kk], sem).wait()

    gates = gate_ref[...]
    y = gates[:, 0:1] * buf[0]
    for kk in range(1, TOP_K):
        y = y + gates[:, kk:kk + 1] * buf[kk]
    out = x_ref[...] + gf_ref[...] * y
    if final:
        out = _rms(out) * nw_ref[...]
    o_ref[...] = out


def _combine(pos_flat, ys, x1, gates, mod, layer, norm_final, final):
    s, d = x1.shape
    tm = min(TM_COMB, s)
    vmem = TOP_K * tm * d * 4 + 6 * tm * d * 4 + (6 << 20)
    return pl.pallas_call(
        functools.partial(_combine_kernel, tm=tm, final=final),
        grid=(s // tm,),
        in_specs=[
            pl.BlockSpec((tm * TOP_K,), lambda i: (i,), memory_space=pltpu.SMEM),
            pl.BlockSpec(memory_space=pl.ANY),
            pl.BlockSpec((tm, d), lambda i: (i, 0)),
            pl.BlockSpec((tm, V7X_LANES), lambda i: (i, 0)),
            pl.BlockSpec((None, 1, d), lambda i: (layer, 0, 5)),
            pl.BlockSpec((1, d), lambda i: (0, 0)),
        ],
        out_specs=pl.BlockSpec((tm, d), lambda i: (i, 0)),
        out_shape=jax.ShapeDtypeStruct((s, d), F32),
        scratch_shapes=[pltpu.VMEM((TOP_K, tm, d), F32), pltpu.SemaphoreType.DMA(())],
        compiler_params=_params(1, vmem),
        name="moe_combine",
    )(pos_flat, ys, x1, gates, mod, norm_final)


def kernel(x, c, positions, ada_w, ada_b, norm_mix, w_in, w_out, lambda_q1, lambda_k1, lambda_q2, lambda_k2,
           subln_w, sb_norm_w, norm_ffn, router_w, router_b, w_gate_up, b_gate_up, w_down, b_down, norm_final):
    batch, s, d = x.shape
    assert batch == 1, "the kernels are written for one sequence"
    depth = ada_w.shape[0]
    n_exp = router_w.shape[-1]
    assert n_exp <= V7X_LANES

    xc = x.reshape(s, d)
    mod = _ada_mod(c.reshape(d, 1), ada_w, ada_b)
    cos, sin = _rope_tables(positions.reshape(s, 1).astype(F32))
    w_in_b = w_in.astype(BF16)
    w_out_b = w_out.astype(BF16)

    n_tiles = (s * TOP_K) // TM_MOE + n_exp
    n_rows = n_tiles * TM_MOE
    pad = V7X_LANES - n_exp
    row = lambda a: a.reshape(1, -1)

    for l in range(depth):
        lam_init = 0.8 - 0.6 * math.exp(-0.3 * l)
        lam = (jnp.exp(jnp.sum(lambda_q1[l] * lambda_k1[l])) - jnp.exp(jnp.sum(lambda_q2[l] * lambda_k2[l]))
               + lam_init).reshape(1).astype(F32)

        qkv = _in_proj(xc, mod, l, row(norm_mix[l]), w_in_b, cos, sin)
        a_mix = _diff_attn(qkv, lam, row(subln_w[l]), 1.0 - lam_init)
        b_mix = _sb_attn(qkv, row(sb_norm_w[l]))

        rw = jnp.pad(router_w[l], ((0, 0), (0, pad)))
        rw_hi = rw.astype(BF16)
        rw_lo = (rw - rw_hi.astype(F32)).astype(BF16)
        rb = jnp.pad(router_b[l], (0, pad), constant_values=NEG_BIG).reshape(1, V7X_LANES)
        x1, hp, eidx, gates, sel = _out_router(a_mix, b_mix, w_out_b, xc, mod, l, row(norm_ffn[l]),
                                               rw_hi, rw_lo, rb)

        rank, counts = _rank(sel)
        counts = counts[0, :n_exp].astype(jnp.int32)
        plan = _moe_plan(counts)
        slot = rank[:, :n_exp] + (plan[0] * TM_MOE).astype(F32)[None, :]
        pos = jnp.take_along_axis(slot, eidx[:, :TOP_K], axis=1).astype(jnp.int32).reshape(-1)

        xs = _scatter_rows(pos, hp, jnp.zeros((n_rows, d // 2), jnp.uint32))
        act = _moe_up(plan, xs, w_gate_up, b_gate_up, l)
        ys = _moe_down(plan, act, w_down, b_down, l)
        xc = _combine(pos, ys, x1, gates, mod, l, row(norm_final), l == depth - 1)

    return xc.reshape(batch, s, d)
```

```python
import functools
import math

import jax
import jax.numpy as jnp
from jax import lax
from jax.experimental import pallas as pl
from jax.experimental.pallas import tpu as pltpu

F32 = jnp.float32
BF16 = jnp.bfloat16

HEAD_DIM = 128
DIFF_QK_DIM = HEAD_DIM // 2
N_MOD = 6
TOP_K = 4
SWIGLU_LIMIT = 7.0
SWIGLU_ALPHA = 1.702
ROPE_THETA = 10000.0
RMS_EPS = 1e-5

V7X_LANES = 128
V7X_VMEM_BYTES = 64 * 1024 * 1024

TM_PROJ = 512
TQ_ATTN = 256
TQ_DIFF = 512
SB_HEADS_PER_STEP = 4
DIFF_HEADS_PER_STEP = 2
TM_OUT = 256
TM_RANK = 256
TM_MOE = 512
TM_COMB = 256
MOE_CHUNK = 1024
MOD_CHUNK = 1024
X_AHEAD = 2
W_SLABS = 4
NEG_BIG = -1e30
SB_EXIT_LOG = -100.0

_NT_DIMS = (((1,), (1,)), ((), ()))


def _params(n_axes, vmem_bytes):
    return pltpu.CompilerParams(
        dimension_semantics=("arbitrary",) * n_axes,
        vmem_limit_bytes=min(int(vmem_bytes), V7X_VMEM_BYTES - 8 * 1024 * 1024),
    )


def _rms(x):
    return x * lax.rsqrt(jnp.mean(x * x, axis=-1, keepdims=True) + RMS_EPS)


def _mod_kernel(c_ref, w_ref, b_ref, o_ref):
    c = c_ref[...]
    ca = c * (1.0 / (1.0 + jnp.exp(-c)))
    o_ref[...] = jnp.sum(ca * w_ref[...], axis=0, keepdims=True) + b_ref[...]


def _ada_mod(c_col, ada_w, ada_b):
    depth, d, n = ada_w.shape
    nc = min(MOD_CHUNK, n)
    return pl.pallas_call(
        _mod_kernel,
        grid=(depth, n // nc),
        in_specs=[
            pl.BlockSpec((d, 1), lambda l, j: (0, 0)),
            pl.BlockSpec((None, d, nc), lambda l, j: (l, 0, j)),
            pl.BlockSpec((None, 1, nc), lambda l, j: (l, 0, j)),
        ],
        out_specs=pl.BlockSpec((None, 1, nc), lambda l, j: (l, 0, j)),
        out_shape=jax.ShapeDtypeStruct((depth, 1, n), F32),
        compiler_params=_params(2, 3 * d * nc * 4 + 4 * d * V7X_LANES * 4),
        name="ada_mod",
    )(c_col, ada_w, ada_b.reshape(depth, 1, n))


def _rope_kernel(pos_ref, invf_ref, sign_ref, cos_ref, sin_ref):
    ang = pos_ref[...] * invf_ref[...]
    cos_ref[...] = jnp.cos(ang)
    sin_ref[...] = jnp.sin(ang) * sign_ref[...]


def _rope_tables(pos_col):
    s = pos_col.shape[0]
    half = DIFF_QK_DIM // 2
    inv_freq = ROPE_THETA ** (-jnp.arange(half, dtype=F32) / half)
    reps = V7X_LANES // half
    invf = jnp.tile(inv_freq, reps).reshape(1, V7X_LANES)
    sign = jnp.tile(jnp.concatenate([-jnp.ones((half,), F32), jnp.ones((half,), F32)]), reps // 2)
    ts = min(1024, s)
    tab = jax.ShapeDtypeStruct((s, V7X_LANES), F32)
    return pl.pallas_call(
        _rope_kernel,
        grid=(s // ts,),
        in_specs=[
            pl.BlockSpec((ts, 1), lambda i: (i, 0)),
            pl.BlockSpec((1, V7X_LANES), lambda i: (0, 0)),
            pl.BlockSpec((1, V7X_LANES), lambda i: (0, 0)),
        ],
        out_specs=[pl.BlockSpec((ts, V7X_LANES), lambda i: (i, 0))] * 2,
        out_shape=[tab, tab],
        compiler_params=_params(1, 32 * 1024 * 1024),
        name="rope_tables",
    )(pos_col, invf, sign.reshape(1, V7X_LANES))


def _in_proj_kernel(x_ref, nw_ref, sc_ref, sh_ref, w_ref, cos_ref, sin_ref, o_ref, h_scr, *, n_heads):
    j = pl.program_id(1)

    @pl.when(j == 0)
    def _():
        h = _rms(x_ref[...]) * nw_ref[...]
        h_scr[...] = (h * (1.0 + sc_ref[...]) + sh_ref[...]).astype(BF16)

    res = jnp.dot(h_scr[...], w_ref[...], preferred_element_type=F32)

    def store(scale, rotary):
        if rotary:
            cos = cos_ref[...]
            sin = sin_ref[...]
            lane = lax.broadcasted_iota(jnp.int32, cos.shape, 1)
            first_half = (lane % DIFF_QK_DIM) < (DIFF_QK_DIM // 2)
        for g in range(n_heads):
            blk = res[:, g * HEAD_DIM:(g + 1) * HEAD_DIM]
            if rotary:
                fwd = pltpu.roll(blk, HEAD_DIM - DIFF_QK_DIM // 2, 1)
                bwd = pltpu.roll(blk, DIFF_QK_DIM // 2, 1)
                blk = blk * cos + jnp.where(first_half, fwd, bwd) * sin
            if scale != 1.0:
                blk = blk * scale
            o_ref[g] = blk.astype(BF16)

    pl.when(j == 0)(lambda: store(DIFF_QK_DIM ** -0.5 * math.log2(math.e), True))
    pl.when(j == 1)(lambda: store(1.0, True))
    pl.when(j == 3)(lambda: store(HEAD_DIM ** -0.5, False))
    pl.when((j == 2) | (j >= 4))(lambda: store(1.0, False))


def _in_proj(x, mod, layer, norm_w, w_in, cos, sin):
    s, d = x.shape
    gw = w_in.shape[-1] // 6
    n_heads = gw // HEAD_DIM
    tm = min(TM_PROJ, s)
    vmem = 2 * tm * d * 4 + tm * d * 2 + 2 * d * gw * 2 + 2 * tm * gw * 2 + 3 * tm * gw * 4 + (4 << 20)
    return pl.pallas_call(
        functools.partial(_in_proj_kernel, n_heads=n_heads),
        grid=(s // tm, 6),
        in_specs=[
            pl.BlockSpec((tm, d), lambda i, j: (i, 0)),
            pl.BlockSpec((1, d), lambda i, j: (0, 0)),
            pl.BlockSpec((None, 1, d), lambda i, j: (layer, 0, 1)),
            pl.BlockSpec((None, 1, d), lambda i, j: (layer, 0, 0)),
            pl.BlockSpec((None, d, gw), lambda i, j: (layer, 0, j)),
            pl.BlockSpec((tm, V7X_LANES), lambda i, j: (i, 0)),
            pl.BlockSpec((tm, V7X_LANES), lambda i, j: (i, 0)),
        ],
        out_specs=pl.BlockSpec((n_heads, tm, HEAD_DIM), lambda i, j: (j, i, 0)),
        out_shape=jax.ShapeDtypeStruct((6 * n_heads, s, HEAD_DIM), BF16),
        scratch_shapes=[pltpu.VMEM((tm, d), BF16)],
        compiler_params=_params(2, vmem),
        name="in_proj",
    )(x, norm_w, mod, mod, w_in, cos, sin)


def _diff_attn_kernel(lam_ref, q_ref, k_ref, v_ref, w_ref, o_ref, m_scr, l_scr, acc_scr, *, tq, out_scale, n_group):
    qi = pl.program_id(1)
    lane = lax.broadcasted_iota(jnp.int32, (tq, HEAD_DIM), 1)
    n_chunks = tq // V7X_LANES
    m_scr[...] = jnp.full(m_scr.shape, NEG_BIG, F32)
    l_scr[...] = jnp.zeros(l_scr.shape, F32)
    acc_scr[...] = jnp.zeros(acc_scr.shape, F32)

    def step(ki, diag):
        start = pl.multiple_of(ki * tq, tq)
        for h in range(n_group):
            q = q_ref[h]
            zero = jnp.zeros_like(q)
            k = k_ref[h, pl.ds(start, tq), :]
            v = v_ref[h, pl.ds(start, tq), :]
            for p in range(2):
                q_sub = jnp.where((lane < DIFF_QK_DIM) == (p == 0), q, zero)
                s = lax.dot_general(q_sub, k, _NT_DIMS, preferred_element_type=F32)
                if diag:
                    row = lax.broadcasted_iota(jnp.int32, s.shape, 0)
                    col = lax.broadcasted_iota(jnp.int32, s.shape, 1)
                    s = jnp.where(col <= row, s, NEG_BIG)
                chunks = [s[:, c * V7X_LANES:(c + 1) * V7X_LANES] for c in range(n_chunks)]
                i = 2 * h + p
                m = m_scr[i]
                m_new = jnp.maximum(m, jnp.max(functools.reduce(jnp.maximum, chunks), axis=-1, keepdims=True))
                alpha = jnp.exp2(m - m_new)
                pe = [jnp.exp2(ch - m_new) for ch in chunks]
                l_scr[i] = alpha * l_scr[i] + functools.reduce(jnp.add, pe)
                pb = jnp.concatenate([x.astype(BF16) for x in pe], axis=-1)
                acc_scr[i] = alpha * acc_scr[i] + jnp.dot(pb, v, preferred_element_type=F32)
                m_scr[i] = m_new

    def body(ki, c):
        step(ki, False)
        return c

    lax.fori_loop(0, qi, body, 0)
    step(qi, True)
    for h in range(n_group):
        den = [jnp.sum(l_scr[2 * h + p], axis=-1, keepdims=True) for p in range(2)]
        a = acc_scr[2 * h] / den[0] - lam_ref[0] * (acc_scr[2 * h + 1] / den[1])
        o_ref[h] = (_rms(a) * (w_ref[...] * out_scale)).astype(BF16)


def _diff_attn(qkv, lam, subln_w, out_scale):
    n_heads = qkv.shape[0] // 6
    s = qkv.shape[1]
    tq = min(TQ_DIFF, s)
    n_group = math.gcd(DIFF_HEADS_PER_STEP, n_heads)
    vmem = 4 * n_group * s * HEAD_DIM * 2 + n_group * (16 * tq * tq * 4 + 6 * tq * V7X_LANES * 4) + (8 << 20)
    kv = lambda base: (lambda hg, i: (base * n_heads // n_group + hg, 0, 0))
    return pl.pallas_call(
        functools.partial(_diff_attn_kernel, tq=tq, out_scale=out_scale, n_group=n_group),
        grid=(n_heads // n_group, s // tq),
        in_specs=[
            pl.BlockSpec(memory_space=pltpu.SMEM),
            pl.BlockSpec((n_group, tq, HEAD_DIM), lambda hg, i: (hg, i, 0)),
            pl.BlockSpec((n_group, s, HEAD_DIM), kv(1)),
            pl.BlockSpec((n_group, s, HEAD_DIM), kv(2)),
            pl.BlockSpec((1, HEAD_DIM), lambda hg, i: (0, 0)),
        ],
        out_specs=pl.BlockSpec((n_group, tq, HEAD_DIM), lambda hg, i: (hg, i, 0)),
        out_shape=jax.ShapeDtypeStruct((n_heads, s, HEAD_DIM), BF16),
        scratch_shapes=[pltpu.VMEM((2 * n_group, tq, V7X_LANES), F32), pltpu.VMEM((2 * n_group, tq, V7X_LANES), F32),
                        pltpu.VMEM((2 * n_group, tq, HEAD_DIM), F32)],
        compiler_params=_params(2, vmem),
        name="diff_attn",
    )(lam, qkv, qkv, qkv, subln_w)


def _sb_attn_kernel(q_ref, k_ref, v_ref, w_ref, o_ref, *, tq, n_group):
    qi = pl.program_id(1)
    n_chunks = tq // V7X_LANES
    row = lax.broadcasted_iota(jnp.int32, (tq, tq), 0)
    col = lax.broadcasted_iota(jnp.int32, (tq, tq), 1)
    strict = col < row
    after = jnp.where(row > col, 1.0, 0.0).astype(BF16)
    ones = jnp.ones((tq, V7X_LANES), BF16)

    def block(h, ki, rem, acc, diag):
        start = pl.multiple_of(ki * tq, tq)
        k = k_ref[h, pl.ds(start, tq), :]
        v = v_ref[h, pl.ds(start, tq), :]
        z = lax.dot_general(q_ref[h], k, _NT_DIMS, preferred_element_type=F32)
        t = jnp.log1p(jnp.exp(-jnp.abs(z)))
        log_beta = jnp.minimum(z, 0.0) - t
        log_1m = -jnp.maximum(z, 0.0) - t
        if diag:
            log_1m = jnp.where(strict, log_1m, 0.0)
        hi = log_1m.astype(BF16)
        lo = (log_1m - hi.astype(F32)).astype(BF16)
        later = (jnp.dot(hi, after, preferred_element_type=F32)
                 + jnp.dot(lo, after, preferred_element_type=F32))
        total = (jnp.dot(hi, ones, preferred_element_type=F32)
                 + jnp.dot(lo, ones, preferred_element_type=F32))
        log_a = log_beta + later
        a = []
        for c in range(n_chunks):
            sl = slice(c * V7X_LANES, (c + 1) * V7X_LANES)
            a_c = jnp.exp(log_a[:, sl] + rem)
            if diag:
                a_c = jnp.where(strict[:, sl], a_c, 0.0)
            a.append(a_c.astype(BF16))
        acc = acc + jnp.dot(jnp.concatenate(a, axis=-1), v, preferred_element_type=F32)
        return rem + total, acc

    zero = jnp.zeros((tq, V7X_LANES), F32)
    state = [block(h, qi, zero, zero, True) for h in range(n_group)]
    rems = tuple(s[0] for s in state)
    accs = tuple(s[1] for s in state)

    def cond(c):
        ki, rems, _ = c
        return jnp.logical_and(ki >= 0, jnp.max(functools.reduce(jnp.maximum, rems)) > SB_EXIT_LOG)

    def body(c):
        ki, rems, accs = c
        state = [block(h, ki, rems[h], accs[h], False) for h in range(n_group)]
        return ki - 1, tuple(s[0] for s in state), tuple(s[1] for s in state)

    _, _, accs = lax.while_loop(cond, body, (qi - 1, rems, accs))
    for h in range(n_group):
        o_ref[h] = (_rms(accs[h]) * w_ref[...]).astype(BF16)


def _sb_attn(qkv, norm_w):
    n_heads = qkv.shape[0] // 6
    s = qkv.shape[1]
    tq = min(TQ_ATTN, s)
    n_group = math.gcd(SB_HEADS_PER_STEP, n_heads)
    vmem = 4 * n_group * s * HEAD_DIM * 2 + n_group * 12 * tq * tq * 4 + (8 << 20)
    blk = lambda base: (lambda hg, i: (base * n_heads // n_group + hg, 0, 0))
    return pl.pallas_call(
        functools.partial(_sb_attn_kernel, tq=tq, n_group=n_group),
        grid=(n_heads // n_group, s // tq),
        in_specs=[
            pl.BlockSpec((n_group, tq, HEAD_DIM), lambda hg, i: (3 * n_heads // n_group + hg, i, 0)),
            pl.BlockSpec((n_group, s, HEAD_DIM), blk(4)),
            pl.BlockSpec((n_group, s, HEAD_DIM), blk(5)),
            pl.BlockSpec((1, HEAD_DIM), lambda hg, i: (0, 0)),
        ],
        out_specs=pl.BlockSpec((n_group, tq, HEAD_DIM), lambda hg, i: (hg, i, 0)),
        out_shape=jax.ShapeDtypeStruct((n_heads, s, HEAD_DIM), BF16),
        compiler_params=_params(2, vmem),
        name="sb_attn",
    )(qkv, qkv, qkv, norm_w)


def _out_router_kernel(a_ref, b_ref, w_ref, x_ref, ga_ref, nw_ref, sc_ref, sh_ref, rwh_ref, rwl_ref, rb_ref,
                       x1_ref, hp_ref, eidx_ref, gate_ref, sel_ref, *, n_heads):
    mix = jnp.concatenate([a_ref[g] for g in range(n_heads)] + [b_ref[g] for g in range(n_heads)], axis=-1)
    y = jnp.dot(mix, w_ref[...], preferred_element_type=F32)
    x1 = x_ref[...] + ga_ref[...] * y
    x1_ref[...] = x1
    h = _rms(x1) * nw_ref[...]
    h = h * (1.0 + sc_ref[...]) + sh_ref[...]
    half = h.shape[-1] // 2
    hp_ref[...] = pltpu.pack_elementwise([h[:, :half], h[:, half:]], packed_dtype=BF16)

    h_hi = h.astype(BF16)
    h_lo = (h - h_hi.astype(F32)).astype(BF16)
    logits = (jnp.dot(h_hi, rwh_ref[...], preferred_element_type=F32)
              + jnp.dot(h_lo, rwh_ref[...], preferred_element_type=F32)
              + jnp.dot(h_hi, rwl_ref[...], preferred_element_type=F32)) + rb_ref[...]

    lane = lax.broadcasted_iota(jnp.int32, logits.shape, 1).astype(F32)
    vals, idxs = [], []
    sel = jnp.zeros(logits.shape, F32)
    for _ in range(TOP_K):
        m = jnp.max(logits, axis=-1, keepdims=True)
        idx = jnp.min(jnp.where(logits == m, lane, float(V7X_LANES)), axis=-1, keepdims=True)
        hit = lane == idx
        vals.append(m)
        idxs.append(idx)
        sel = jnp.where(hit, 1.0, sel)
        logits = jnp.where(hit, -3e38, logits)
    exps = [jnp.exp(v - vals[0]) for v in vals]
    denom = exps[0] + exps[1] + exps[2] + exps[3]
    eidx = jnp.zeros(logits.shape, F32)
    gate = jnp.zeros(logits.shape, F32)
    for kk in range(TOP_K):
        eidx = jnp.where(lane == float(kk), idxs[kk], eidx)
        gate = jnp.where(lane == float(kk), exps[kk] / denom, gate)
    eidx_ref[...] = eidx.astype(jnp.int32)
    gate_ref[...] = gate
    sel_ref[...] = sel.astype(BF16)


def _out_router(a_mix, b_mix, w_out, x, mod, layer, norm_w, rw_hi, rw_lo, rb):
    n_heads, s, _ = a_mix.shape
    d = x.shape[1]
    tm = min(TM_OUT, s)
    vmem = 2 * d * d * 2 + 10 * tm * d * 4 + (8 << 20)
    row = lambda i: (i, 0)
    fixed = lambda i: (0, 0)
    lanes = jax.ShapeDtypeStruct((s, V7X_LANES), F32)
    return pl.pallas_call(
        functools.partial(_out_router_kernel, n_heads=n_heads),
        grid=(s // tm,),
        in_specs=[
            pl.BlockSpec((n_heads, tm, HEAD_DIM), lambda i: (0, i, 0)),
            pl.BlockSpec((n_heads, tm, HEAD_DIM), lambda i: (0, i, 0)),
            pl.BlockSpec((None, d, d), lambda i: (layer, 0, 0)),
            pl.BlockSpec((tm, d), row),
            pl.BlockSpec((None, 1, d), lambda i: (layer, 0, 2)),
            pl.BlockSpec((1, d), fixed),
            pl.BlockSpec((None, 1, d), lambda i: (layer, 0, 4)),
            pl.BlockSpec((None, 1, d), lambda i: (layer, 0, 3)),
            pl.BlockSpec((d, V7X_LANES), fixed),
            pl.BlockSpec((d, V7X_LANES), fixed),
            pl.BlockSpec((1, V7X_LANES), fixed),
        ],
        out_specs=[
            pl.BlockSpec((tm, d), row),
            pl.BlockSpec((tm, d // 2), row),
            pl.BlockSpec((tm, V7X_LANES), row),
            pl.BlockSpec((tm, V7X_LANES), row),
            pl.BlockSpec((tm, V7X_LANES), row),
        ],
        out_shape=[
            jax.ShapeDtypeStruct((s, d), F32),
            jax.ShapeDtypeStruct((s, d // 2), jnp.uint32),
            jax.ShapeDtypeStruct((s, V7X_LANES), jnp.int32),
            lanes,
            jax.ShapeDtypeStruct((s, V7X_LANES), BF16),
        ],
        compiler_params=_params(1, vmem),
        name="out_router",
    )(a_mix, b_mix, w_out, x, mod, norm_w, mod, mod, rw_hi, rw_lo, rb)


def _rank_kernel(sel_ref, rank_ref, count_ref, run_scr):
    i = pl.program_id(0)

    @pl.when(i == 0)
    def _():
        run_scr[...] = jnp.zeros_like(run_scr)

    sel = sel_ref[...]
    tm = sel.shape[0]
    row = lax.broadcasted_iota(jnp.int32, (tm, tm), 0)
    col = lax.broadcasted_iota(jnp.int32, (tm, tm), 1)
    before = jnp.where(col < row, 1.0, 0.0).astype(BF16)
    run = run_scr[...]
    rank_ref[...] = jnp.dot(before, sel, preferred_element_type=F32) + run
    run = run + jnp.sum(sel.astype(F32), axis=0, keepdims=True)
    run_scr[...] = run
    count_ref[...] = run


def _rank(sel):
    s = sel.shape[0]
    tm = min(TM_RANK, s)
    return pl.pallas_call(
        _rank_kernel,
        grid=(s // tm,),
        in_specs=[pl.BlockSpec((tm, V7X_LANES), lambda i: (i, 0))],
        out_specs=[pl.BlockSpec((tm, V7X_LANES), lambda i: (i, 0)),
                   pl.BlockSpec((1, V7X_LANES), lambda i: (0, 0))],
        out_shape=[jax.ShapeDtypeStruct((s, V7X_LANES), F32), jax.ShapeDtypeStruct((1, V7X_LANES), F32)],
        scratch_shapes=[pltpu.VMEM((1, V7X_LANES), F32)],
        compiler_params=_params(1, 16 * 1024 * 1024),
        name="route_rank",
    )(sel)


def _scatter_kernel(pos_ref, h_ref, xs_in_ref, xs_ref, sem, *, tm):
    del xs_in_ref

    def issue(r, _):
        for kk in range(TOP_K):
            dst = pos_ref[r * TOP_K + kk]
            pltpu.make_async_copy(h_ref.at[pl.ds(r, 1)], xs_ref.at[pl.ds(dst, 1)], sem).start()
        return ()

    lax.fori_loop(0, tm, issue, ())
    for _ in range(TOP_K):
        pltpu.make_async_copy(h_ref, xs_ref.at[pl.ds(0, tm)], sem).wait()


def _scatter_rows(pos_flat, hp, xs_zero):
    s, w = hp.shape
    tm = min(TM_COMB, s)
    return pl.pallas_call(
        functools.partial(_scatter_kernel, tm=tm),
        grid=(s // tm,),
        in_specs=[
            pl.BlockSpec((tm * TOP_K,), lambda i: (i,), memory_space=pltpu.SMEM),
            pl.BlockSpec((tm, w), lambda i: (i, 0)),
            pl.BlockSpec(memory_space=pl.ANY),
        ],
        out_specs=pl.BlockSpec(memory_space=pl.ANY),
        out_shape=jax.ShapeDtypeStruct(xs_zero.shape, xs_zero.dtype),
        scratch_shapes=[pltpu.SemaphoreType.DMA(())],
        input_output_aliases={2: 0},
        compiler_params=_params(1, 16 * 1024 * 1024),
        name="moe_scatter",
    )(pos_flat, hp, xs_zero)


def _row_tiles(n, t0, col0, x_hbm, o_hbm, xbuf, obuf, isem, osem, compute, before_loop, each_tile, *, tm, out_cols):
    def x_copy(t, slot):
        r0 = pl.multiple_of((t0 + t) * tm, tm)
        return pltpu.make_async_copy(x_hbm.at[pl.ds(r0, tm)], xbuf.at[slot], isem.at[slot])

    def o_copy(t, slot):
        r0 = pl.multiple_of((t0 + t) * tm, tm)
        return pltpu.make_async_copy(obuf.at[slot], o_hbm.at[pl.ds(r0, tm), pl.ds(col0, out_cols)], osem.at[slot])

    for a in range(X_AHEAD):
        @pl.when(n > a)
        def _():
            x_copy(a, a).start()

    before_loop()

    def body(t, carry):
        slot = lax.rem(t, 2)

        @pl.when(t + X_AHEAD < n)
        def _():
            x_copy(t + X_AHEAD, lax.rem(t + X_AHEAD, X_AHEAD + 1)).start()

        each_tile(t)
        xslot = lax.rem(t, X_AHEAD + 1)
        x_copy(t, xslot).wait()

        @pl.when(t >= 2)
        def _():
            o_copy(t - 2, slot).wait()

        obuf[slot] = compute(xbuf[xslot])
        o_copy(t, slot).start()
        return carry

    lax.fori_loop(0, n, body, 0)

    @pl.when(n >= 2)
    def _():
        o_copy(n - 2, lax.rem(n, 2)).wait()

    @pl.when(n >= 1)
    def _():
        o_copy(n - 1, lax.rem(n - 1, 2)).wait()


def _zero_fill_rows(first_tile, n_tiles, o_hbm, zbuf, zsem, *, tm):
    zbuf[...] = jnp.zeros_like(zbuf)

    def z_copy(t):
        r0 = pl.multiple_of(t * tm, tm)
        return pltpu.make_async_copy(zbuf, o_hbm.at[pl.ds(r0, tm)], zsem)

    def start(t, carry):
        z_copy(t).start()
        return carry

    def wait(t, carry):
        z_copy(t).wait()
        return carry

    lax.fori_loop(first_tile, n_tiles, start, 0)
    lax.fori_loop(first_tile, n_tiles, wait, 0)


def _expert_groups(layer, start_ref, ntile_ref, total_ref, w_hbm, x_hbm, o_hbm, wbuf, wb_scr, xbuf, obuf, zbuf,
                   wsem, isem, osem, zsem, compute, *, n_chunks, n_groups, n_tiles, tm, out_cols):
    g = pl.program_id(0)
    d_in, nc = wb_scr.shape
    rows = d_in // W_SLABS

    def tiles_of(gg):
        return ntile_ref[gg // n_chunks]

    def w_copy(gg, k):
        col = pl.multiple_of(lax.rem(gg, n_chunks) * nc, nc)
        r0 = pl.multiple_of(k * rows, rows)
        slot = lax.rem(gg, 2)
        return pltpu.make_async_copy(w_hbm.at[layer, gg // n_chunks, pl.ds(r0, rows), pl.ds(col, nc)],
                                     wbuf.at[slot, pl.ds(r0, rows)], wsem.at[slot])

    @pl.when(jnp.logical_and(g == 0, tiles_of(0) > 0))
    def _():
        for k in range(W_SLABS):
            w_copy(0, k).start()

    @pl.when(g < n_groups)
    def _():
        n = tiles_of(g)
        nxt = jnp.minimum(g + 1, n_groups - 1)
        fetch_next = jnp.logical_and(g + 1 < n_groups, tiles_of(nxt) > 0)
        col0 = pl.multiple_of(lax.rem(g, n_chunks) * out_cols, out_cols)

        def before_loop():
            @pl.when(n > 0)
            def _():
                for k in range(W_SLABS):
                    w_copy(g, k).wait()
                wb_scr[...] = wbuf[lax.rem(g, 2)].astype(BF16)

        def each_tile(t):
            @pl.when(jnp.logical_and(fetch_next, t < W_SLABS))
            def _():
                w_copy(nxt, t).start()

        _row_tiles(n, start_ref[g // n_chunks], col0, x_hbm, o_hbm, xbuf, obuf, isem, osem, compute,
                   before_loop, each_tile, tm=tm, out_cols=out_cols)

        for k in range(W_SLABS):
            @pl.when(jnp.logical_and(fetch_next, n <= k))
            def _():
                w_copy(nxt, k).start()

    @pl.when(g == n_groups)
    def _():
        _zero_fill_rows(total_ref[0], n_tiles, o_hbm, zbuf, zsem, tm=tm)


def _moe_up_kernel(start_ref, ntile_ref, total_ref, b_ref, w_hbm, x_hbm, o_hbm,
                   wbuf, wb_scr, xbuf, obuf, zbuf, wsem, isem, osem, zsem, *, layer, n_chunks, n_groups, n_tiles, tm):
    nc = wb_scr.shape[-1]

    def compute(xp):
        half = xp.shape[-1]
        x_lo = pltpu.unpack_elementwise(xp, index=0, packed_dtype=BF16, unpacked_dtype=F32).astype(BF16)
        x_hi = pltpu.unpack_elementwise(xp, index=1, packed_dtype=BF16, unpacked_dtype=F32).astype(BF16)
        gu = (jnp.dot(x_lo, wb_scr[0:half, :], preferred_element_type=F32)
              + jnp.dot(x_hi, wb_scr[half:2 * half, :], preferred_element_type=F32)) + b_ref[...]
        gate = jnp.minimum(gu, SWIGLU_LIMIT)
        glu = gate * (1.0 / (1.0 + jnp.exp(-SWIGLU_ALPHA * gate)))
        up1 = jnp.clip(gu, -SWIGLU_LIMIT, SWIGLU_LIMIT) + 1.0
        pair = 2 * V7X_LANES
        r = lax.broadcasted_iota(jnp.int32, (pair, V7X_LANES), 0)
        c = lax.broadcasted_iota(jnp.int32, (pair, V7X_LANES), 1)
        even = jnp.where(r == 2 * c, 1.0, 0.0).astype(BF16)
        outs = []
        for q in range(gu.shape[-1] // pair):
            sl = slice(q * pair, (q + 1) * pair)
            prod = glu[:, sl] * pltpu.roll(up1[:, sl], pair - 1, 1)
            outs.append(jnp.dot(prod.astype(BF16), even, preferred_element_type=F32))
        return jnp.concatenate(outs, axis=-1).astype(BF16)

    _expert_groups(layer, start_ref, ntile_ref, total_ref, w_hbm, x_hbm, o_hbm, wbuf, wb_scr, xbuf, obuf, zbuf,
                   wsem, isem, osem, zsem, compute, n_chunks=n_chunks, n_groups=n_groups, n_tiles=n_tiles, tm=tm,
                   out_cols=nc // 2)


def _group_specs(layer, n_chunks, n_groups, nc):
    def b_map(g, *_):
        gg = jnp.minimum(g, n_groups - 1)
        return (layer, gg // n_chunks, 0, lax.rem(gg, n_chunks))
    any_spec = pl.BlockSpec(memory_space=pl.ANY)
    return [pl.BlockSpec((None, None, 1, nc), b_map), any_spec, any_spec]


def _moe_up(plan, xs, w_gate_up, b_gate_up, layer):
    p, half = xs.shape
    n_exp, d, f2 = w_gate_up.shape[1:]
    nc = min(MOE_CHUNK, f2)
    n_chunks = f2 // nc
    n_groups = n_exp * n_chunks
    tm = TM_MOE
    vmem = 2 * d * nc * 4 + d * nc * 2 + 2 * tm * half * 4 + 3 * tm * f2 + 8 * tm * nc * 4 + (6 << 20)
    grid_spec = pltpu.PrefetchScalarGridSpec(
        num_scalar_prefetch=3,
        grid=(n_groups + 1,),
        in_specs=_group_specs(layer, n_chunks, n_groups, nc),
        out_specs=pl.BlockSpec(memory_space=pl.ANY),
        scratch_shapes=[
            pltpu.VMEM((2, d, nc), F32),
            pltpu.VMEM((d, nc), BF16),
            pltpu.VMEM((X_AHEAD + 1, tm, half), jnp.uint32),
            pltpu.VMEM((2, tm, nc // 2), BF16),
            pltpu.VMEM((tm, f2 // 2), BF16),
            pltpu.SemaphoreType.DMA((2,)), pltpu.SemaphoreType.DMA((X_AHEAD + 1,)), pltpu.SemaphoreType.DMA((2,)),
            pltpu.SemaphoreType.DMA(()),
        ],
    )
    return pl.pallas_call(
        functools.partial(_moe_up_kernel, layer=layer, n_chunks=n_chunks, n_groups=n_groups, n_tiles=p // tm, tm=tm),
        grid_spec=grid_spec,
        out_shape=jax.ShapeDtypeStruct((p, f2 // 2), BF16),
        compiler_params=_params(1, vmem),
        name="moe_up",
    )(*plan, b_gate_up.reshape(b_gate_up.shape[0], n_exp, 1, f2), w_gate_up, xs)


def _moe_down_kernel(start_ref, ntile_ref, total_ref, b_ref, w_hbm, x_hbm, o_hbm,
                     wbuf, wb_scr, xbuf, obuf, zbuf, wsem, isem, osem, zsem, *, layer, n_chunks, n_groups, n_tiles, tm):
    def compute(a):
        return jnp.dot(a, wb_scr[...], preferred_element_type=F32) + b_ref[...]

    _expert_groups(layer, start_ref, ntile_ref, total_ref, w_hbm, x_hbm, o_hbm, wbuf, wb_scr, xbuf, obuf, zbuf,
                   wsem, isem, osem, zsem, compute, n_chunks=n_chunks, n_groups=n_groups, n_tiles=n_tiles, tm=tm,
                   out_cols=wb_scr.shape[-1])


def _moe_down(plan, act, w_down, b_down, layer):
    p, f = act.shape
    n_exp, _, d = w_down.shape[1:]
    nc = min(MOE_CHUNK, d)
    n_chunks = d // nc
    n_groups = n_exp * n_chunks
    tm = TM_MOE
    vmem = 2 * f * nc * 4 + f * nc * 2 + 2 * tm * f * 2 + tm * d * 4 + 5 * tm * nc * 4 + (6 << 20)
    grid_spec = pltpu.PrefetchScalarGridSpec(
        num_scalar_prefetch=3,
        grid=(n_groups + 1,),
        in_specs=_group_specs(layer, n_chunks, n_groups, nc),
        out_specs=pl.BlockSpec(memory_space=pl.ANY),
        scratch_shapes=[
            pltpu.VMEM((2, f, nc), F32),
            pltpu.VMEM((f, nc), BF16),
            pltpu.VMEM((X_AHEAD + 1, tm, f), BF16),
            pltpu.VMEM((2, tm, nc), F32),
            pltpu.VMEM((tm, d), F32),
            pltpu.SemaphoreType.DMA((2,)), pltpu.SemaphoreType.DMA((X_AHEAD + 1,)), pltpu.SemaphoreType.DMA((2,)),
            pltpu.SemaphoreType.DMA(()),
        ],
    )
    return pl.pallas_call(
        functools.partial(_moe_down_kernel, layer=layer, n_chunks=n_chunks, n_groups=n_groups, n_tiles=p // tm,
                          tm=tm),
        grid_spec=grid_spec,
        out_shape=jax.ShapeDtypeStruct((p, d), F32),
        compiler_params=_params(1, vmem),
        name="moe_down",
    )(*plan, b_down.reshape(b_down.shape[0], n_exp, 1, d), w_down, act)


def _moe_plan(counts):
    tiles_per = (counts + TM_MOE - 1) // TM_MOE
    tile_end = jnp.cumsum(tiles_per)
    tile_start = tile_end - tiles_per
    i32 = lambda a: a.astype(jnp.int32)
    return i32(tile_start), i32(tiles_per), i32(tile_end[-1:])


def _combine_kernel(pos_ref, ys_ref, x_ref, gate_ref, gf_ref, nw_ref, o_ref, buf, sem, *, tm, final):
    def issue(r, _):
        for kk in range(TOP_K):
            src = pos_ref[r * TOP_K + kk]
            pltpu.make_async_copy(ys_ref.at[pl.ds(src, 1)], buf.at[kk, pl.ds(r, 1)], sem).start()
        return ()

    lax.fori_loop(0, tm, issue, ())
    for kk in range(TOP_K):
        pltpu.make_async_copy(ys_ref.at[pl.ds(0, tm)], buf.at[kk], sem).wait()

    gates = gate_ref[...]
    y = gates[:, 0:1] * buf[0]
    for kk in range(1, TOP_K):
        y = y + gates[:, kk:kk + 1] * buf[kk]
    out = x_ref[...] + gf_ref[...] * y
    if final:
        out = _rms(out) * nw_ref[...]
    o_ref[...] = out


def _combine(pos_flat, ys, x1, gates, mod, layer, norm_final, final):
    s, d = x1.shape
    tm = min(TM_COMB, s)
    vmem = TOP_K * tm * d * 4 + 6 * tm * d * 4 + (6 << 20)
    return pl.pallas_call(
        functools.partial(_combine_kernel, tm=tm, final=final),
        grid=(s // tm,),
        in_specs=[
            pl.BlockSpec((tm * TOP_K,), lambda i: (i,), memory_space=pltpu.SMEM),
            pl.BlockSpec(memory_space=pl.ANY),
            pl.BlockSpec((tm, d), lambda i: (i, 0)),
            pl.BlockSpec((tm, V7X_LANES), lambda i: (i, 0)),
            pl.BlockSpec((None, 1, d), lambda i: (layer, 0, 5)),
            pl.BlockSpec((1, d), lambda i: (0, 0)),
        ],
        out_specs=pl.BlockSpec((tm, d), lambda i: (i, 0)),
        out_shape=jax.ShapeDtypeStruct((s, d), F32),
        scratch_shapes=[pltpu.VMEM((TOP_K, tm, d), F32), pltpu.SemaphoreType.DMA(())],
        compiler_params=_params(1, vmem),
        name="moe_combine",
    )(pos_flat, ys, x1, gates, mod, norm_final)


def kernel(x, c, positions, ada_w, ada_b, norm_mix, w_in, w_out, lambda_q1, lambda_k1, lambda_q2, lambda_k2,
           subln_w, sb_norm_w, norm_ffn, router_w, router_b, w_gate_up, b_gate_up, w_down, b_down, norm_final):
    batch, s, d = x.shape
    assert batch == 1, "the kernels are written for one sequence"
    depth = ada_w.shape[0]
    n_exp = router_w.shape[-1]
    assert n_exp <= V7X_LANES

    xc = x.reshape(s, d)
    mod = _ada_mod(c.reshape(d, 1), ada_w, ada_b)
    cos, sin = _rope_tables(positions.reshape(s, 1).astype(F32))
    w_in_b = w_in.astype(BF16)
    w_out_b = w_out.astype(BF16)

    n_tiles = (s * TOP_K) // TM_MOE + n_exp
    n_rows = n_tiles * TM_MOE
    pad = V7X_LANES - n_exp
    row = lambda a: a.reshape(1, -1)

    for l in range(depth):
        lam_init = 0.8 - 0.6 * math.exp(-0.3 * l)
        lam = (jnp.exp(jnp.sum(lambda_q1[l] * lambda_k1[l])) - jnp.exp(jnp.sum(lambda_q2[l] * lambda_k2[l]))
               + lam_init).reshape(1).astype(F32)

        qkv = _in_proj(xc, mod, l, row(norm_mix[l]), w_in_b, cos, sin)
        a_mix = _diff_attn(qkv, lam, row(subln_w[l]), 1.0 - lam_init)
        b_mix = _sb_attn(qkv, row(sb_norm_w[l]))

        rw = jnp.pad(router_w[l], ((0, 0), (0, pad)))
        rw_hi = rw.astype(BF16)
        rw_lo = (rw - rw_hi.astype(F32)).astype(BF16)
        rb = jnp.pad(router_b[l], (0, pad), constant_values=NEG_BIG).reshape(1, V7X_LANES)
        x1, hp, eidx, gates, sel = _out_router(a_mix, b_mix, w_out_b, xc, mod, l, row(norm_ffn[l]),
                                               rw_hi, rw_lo, rb)

        rank, counts = _rank(sel)
        counts = counts[0, :n_exp].astype(jnp.int32)
        plan = _moe_plan(counts)
        slot = rank[:, :n_exp] + (plan[0] * TM_MOE).astype(F32)[None, :]
        pos = jnp.take_along_axis(slot, eidx[:, :TOP_K], axis=1).astype(jnp.int32).reshape(-1)

        xs = _scatter_rows(pos, hp, jnp.zeros((n_rows, d // 2), jnp.uint32))
        act = _moe_up(plan, xs, w_gate_up, b_gate_up, l)
        ys = _moe_down(plan, act, w_down, b_down, l)
        xc = _combine(pos, ys, x1, gates, mod, l, row(norm_final), l == depth - 1)

    return xc.reshape(batch, s, d)
```
